```python
import jax
import jax.numpy as jnp
from jax import lax
import numpy as np

D_MODEL = 1024
BATCH = 8
SEQ = 2048
DEPTH = 4
DEC_BATCH = 128
DEC_SEQ = 1
PAST_LEN = 16384
PAGE_SIZE = 128

F32 = jnp.float32
N_EVEN = (DEPTH + 1) // 2
N_ODD = DEPTH // 2

SSD_D_INNER = D_MODEL
SSD_HEAD_DIM = 64
SSD_HEADS = SSD_D_INNER // SSD_HEAD_DIM
SSD_GROUPS = 2
SSD_STATE = 128
SSD_CONV = 4
SSD_CONV_CH = SSD_D_INNER + 2 * SSD_GROUPS * SSD_STATE
RET_HEADS = 8
RET_QK_DIM = 64
RET_V_DIM = 128
RET_QK_WIDTH = RET_HEADS * RET_QK_DIM
RET_V_WIDTH = RET_HEADS * RET_V_DIM
ROPE_BASE = 10000.0
EVEN_SPLITS = (SSD_D_INNER, SSD_CONV_CH, SSD_HEADS, RET_QK_WIDTH, RET_QK_WIDTH, RET_V_WIDTH, RET_V_WIDTH)
EVEN_IN = sum(EVEN_SPLITS)
EVEN_MIX = SSD_D_INNER + RET_V_WIDTH
POOL_WIDTH = D_MODEL // 2
POOL_WINDOWS = (2, 4, 8, 16)
POOL_GROUPS = len(POOL_WINDOWS)
POOL_GROUP_DIM = POOL_WIDTH // POOL_GROUPS
POOL_STATE = max(POOL_WINDOWS) - 1
CONV_WIDTH = D_MODEL // 2
CONV_K = 31
ODD_IN = POOL_WIDTH + 2 * CONV_WIDTH
ODD_MIX = POOL_WIDTH + CONV_WIDTH
MEM_LEN = 256
X_HEADS = 4
X_HEAD_DIM = D_MODEL // X_HEADS
D_FF = 2816
N_EXPERTS = 8
TOP_K = 2
D_FF_EXPERT = 1792

CHUNK = 128
EPS = 1e-6

kernel_name = 'hybrid_ssd_retention_pool_conformer_decoder_step'


def rmsnorm(x, g):
    xf = x.astype(F32)
    y = xf * lax.rsqrt(jnp.mean(xf * xf, axis=-1, keepdims=True) + EPS)
    return (y * g.astype(F32)).astype(x.dtype)


def rms_only(x):
    return x * lax.rsqrt(jnp.mean(x * x, axis=-1, keepdims=True) + EPS)


def rotary(x, pos):
    half = x.shape[-1] // 2
    inv = ROPE_BASE ** (-jnp.arange(half, dtype=F32) / half)
    ang = pos[:, None] * inv[None, :]
    cos = jnp.cos(ang)[None, :, None, :]
    sin = jnp.sin(ang)[None, :, None, :]
    x1, x2 = x[..., :half], x[..., half:]
    return jnp.concatenate([x1 * cos - x2 * sin, x1 * sin + x2 * cos], axis=-1)


def retention_log_decay():
    return jnp.log1p(-jnp.exp2(-5.0 - jnp.arange(RET_HEADS, dtype=F32)))


def causal_dwconv(x_full, w, b):
    c = x_full.shape[-1]
    y = lax.conv_general_dilated(x_full, w[:, None, :].astype(x_full.dtype), (1,), 'VALID',
                                 dimension_numbers=('NWC', 'WIO', 'NWC'), feature_group_count=c)
    return y + b.astype(y.dtype)


def chunked_decay_scan(q, k, v, log_a, h0):
    bsz, seq, nh, dk = q.shape
    dv = v.shape[-1]
    cs = CHUNK if seq % CHUNK == 0 else seq
    nc = seq // cs
    q = q.astype(F32).reshape(bsz, nc, cs, nh, dk)
    k = k.astype(F32).reshape(bsz, nc, cs, nh, dk)
    v = v.astype(F32).reshape(bsz, nc, cs, nh, dv)
    acum = jnp.cumsum(log_a.astype(F32).reshape(bsz, nc, cs, nh), axis=2)
    idx = jnp.arange(cs)
    causal = (idx[:, None] >= idx[None, :])[None, None, :, :, None]
    diff = acum[:, :, :, None, :] - acum[:, :, None, :, :]
    decay = jnp.where(causal, jnp.exp(jnp.minimum(diff, 0.0)), 0.0)
    scores = jnp.einsum('bcihk,bcjhk->bcijh', q, k) * decay
    y_intra = jnp.einsum('bcijh,bcjhv->bcihv', scores, v)
    total = acum[:, :, -1]
    w_end = jnp.exp(total[:, :, None, :] - acum)
    chunk_kv = jnp.einsum('bcjh,bcjhk,bcjhv->bchkv', w_end, k, v)

    def step(h, inp):
        dec, kv = inp
        return jnp.exp(dec)[:, :, None, None] * h + kv, h

    h_last, h_start = lax.scan(step, h0.astype(F32),
                               (jnp.moveaxis(total, 1, 0), jnp.moveaxis(chunk_kv, 1, 0)))
    h_start = jnp.moveaxis(h_start, 0, 1)
    y_inter = jnp.einsum('bcihk,bchkv->bcihv', q, h_start) * jnp.exp(acum)[..., None]
    return (y_intra + y_inter).reshape(bsz, seq, nh, dv), h_last


def even_mixer(h, pos0, conv_st, ssm_st, ret_st, w_in, conv_w, conv_b, dt_bias, a_log, d_skip, gnorm, w_out):
    bsz, seq, _ = h.shape
    offs = [int(o) for o in np.cumsum(EVEN_SPLITS)[:-1]]
    z, xbc, dt, q, k, v, g = jnp.split(h @ w_in, offs, axis=-1)
    xbc_full = jnp.concatenate([conv_st.astype(xbc.dtype), xbc], axis=1)
    new_conv = xbc_full[:, -(SSD_CONV - 1):]
    xbc = jax.nn.silu(causal_dwconv(xbc_full, conv_w, conv_b).astype(F32))
    xs, bm, cm = jnp.split(xbc, [SSD_D_INNER, SSD_D_INNER + SSD_GROUPS * SSD_STATE], axis=-1)
    xs = xs.reshape(bsz, seq, SSD_HEADS, SSD_HEAD_DIM)
    rep = SSD_HEADS // SSD_GROUPS
    bm = jnp.repeat(bm.reshape(bsz, seq, SSD_GROUPS, SSD_STATE), rep, axis=2)
    cm = jnp.repeat(cm.reshape(bsz, seq, SSD_GROUPS, SSD_STATE), rep, axis=2)
    dt = jax.nn.softplus(dt.astype(F32) + dt_bias.astype(F32))
    a = -jnp.exp(a_log.astype(F32))
    y, new_ssm = chunked_decay_scan(cm, bm, xs * dt[..., None], dt * a, ssm_st)
    y = y + d_skip.astype(F32)[:, None] * xs
    y = y.reshape(bsz, seq, SSD_D_INNER) * jax.nn.silu(z.astype(F32))
    y = rms_only(y.reshape(bsz, seq, SSD_GROUPS, -1)).reshape(bsz, seq, SSD_D_INNER) * gnorm.astype(F32)
    pos = pos0 + jnp.arange(seq, dtype=F32)
    q = rotary(q.astype(F32).reshape(bsz, seq, RET_HEADS, RET_QK_DIM), pos)
    k = rotary(k.astype(F32).reshape(bsz, seq, RET_HEADS, RET_QK_DIM), pos) * (RET_QK_DIM ** -0.5)
    v = v.astype(F32).reshape(bsz, seq, RET_HEADS, RET_V_DIM)
    log_g = jnp.broadcast_to(retention_log_decay(), (bsz, seq, RET_HEADS))
    o, new_ret = chunked_decay_scan(q, k, v, log_g, ret_st)
    o = rms_only(o).reshape(bsz, seq, RET_V_WIDTH) * jax.nn.silu(g.astype(F32))
    mixed = jnp.concatenate([y, o], axis=-1).astype(h.dtype)
    return (mixed @ w_out, new_conv, new_ssm.astype(ssm_st.dtype), new_ret.astype(ret_st.dtype))


def odd_mixer(h, pos0, pool_st, conv_st, w_in, pool_w, pool_scale, cw, cb, cng, cnb, w_out):
    bsz, seq, _ = h.shape
    u, ga, gb = jnp.split(h @ w_in, [POOL_WIDTH, POOL_WIDTH + CONV_WIDTH], axis=-1)
    u_full = jnp.concatenate([pool_st.astype(u.dtype), u], axis=1)
    new_pool = u_full[:, -POOL_STATE:]
    uf = u_full.astype(F32)
    csum = jnp.concatenate([jnp.zeros_like(uf[:, :1]), jnp.cumsum(uf, axis=1)], axis=1)
    end = csum[:, POOL_STATE + 1:]
    pos = pos0 + jnp.arange(seq, dtype=F32)
    parts = []
    for gi, w in enumerate(POOL_WINDOWS):
        sl = slice(gi * POOL_GROUP_DIM, (gi + 1) * POOL_GROUP_DIM)
        start = csum[:, POOL_STATE + 1 - w:POOL_STATE + 1 - w + seq, sl]
        cnt = jnp.minimum(pos + 1.0, float(w))[None, :, None]
        parts.append((end[..., sl] - start) / cnt)
    pooled = jnp.concatenate(parts, axis=-1) - u.astype(F32)
    pooled = jnp.einsum('blgc,gcd->blgd', pooled.reshape(bsz, seq, POOL_GROUPS, POOL_GROUP_DIM),
                        pool_w.astype(F32)).reshape(bsz, seq, POOL_WIDTH) * pool_scale.astype(F32)
    glu = ga * jax.nn.sigmoid(gb)
    g_full = jnp.concatenate([conv_st.astype(glu.dtype), glu], axis=1)
    new_conv = g_full[:, -(CONV_K - 1):]
    c = causal_dwconv(g_full, cw, cb).astype(F32)
    mu = jnp.mean(c, axis=-1, keepdims=True)
    var = jnp.mean(jnp.square(c - mu), axis=-1, keepdims=True)
    c = jax.nn.silu((c - mu) * lax.rsqrt(var + EPS) * cng.astype(F32) + cnb.astype(F32))
    mixed = jnp.concatenate([pooled, c], axis=-1).astype(h.dtype)
    return (mixed @ w_out, new_pool, new_conv)


def mem_kv(mem, g, wk, wv):
    bsz, m_len, _ = mem.shape
    m = rmsnorm(mem, g)
    return ((m @ wk).reshape(bsz, m_len, X_HEADS, X_HEAD_DIM), (m @ wv).reshape(bsz, m_len, X_HEADS, X_HEAD_DIM))


def cross_attn(h, mk, mv, wq, wo):
    bsz, seq, _ = h.shape
    q = (h @ wq).reshape(bsz, seq, X_HEADS, X_HEAD_DIM).astype(F32)
    s = jnp.einsum('blhd,bmhd->bhlm', q, mk.astype(F32)) * (X_HEAD_DIM ** -0.5)
    p = jax.nn.softmax(s, axis=-1)
    o = jnp.einsum('bhlm,bmhd->blhd', p, mv.astype(F32)).reshape(bsz, seq, X_HEADS * X_HEAD_DIM)
    return o.astype(h.dtype) @ wo


def swiglu(h, w1, w3, w2):
    return (jax.nn.silu(h @ w1) * (h @ w3)) @ w2


def moe(h, rw, rb, w1, w3, w2):
    bsz, seq, d = h.shape
    t = h.reshape(-1, d)
    logits = t.astype(F32) @ rw.astype(F32) + rb.astype(F32)
    top_v, top_i = lax.top_k(logits, TOP_K)
    gates = jax.nn.softmax(top_v, axis=-1)
    combine = jnp.sum(jax.nn.one_hot(top_i, N_EXPERTS, dtype=F32) * gates[..., None], axis=1)
    out = jnp.zeros((t.shape[0], d), F32)
    for e in range(N_EXPERTS):
        out = out + combine[:, e:e + 1] * swiglu(t, w1[e], w3[e], w2[e]).astype(F32)
    return out.astype(h.dtype).reshape(bsz, seq, d)


def trunk(x, pos0, mem_k, mem_v, ssm, ssm_conv, ret, pool, conv, p):
    n_ssm, n_ssm_conv, n_ret, n_pool, n_conv = [], [], [], [], []
    for l in range(DEPTH):
        i = l // 2
        h = rmsnorm(x, p['norm_mix'][l])
        if l % 2 == 0:
            m, c_st, s_st, r_st = even_mixer(h, pos0, ssm_conv[i], ssm[i], ret[i], p['w_in_even'][i],
                                             p['ssd_conv_w'][i], p['ssd_conv_b'][i], p['ssd_dt_bias'][i],
                                             p['ssd_a_log'][i], p['ssd_d'][i], p['ssd_norm'][i], p['w_out_even'][i])
            n_ssm_conv.append(c_st)
            n_ssm.append(s_st)
            n_ret.append(r_st)
        else:
            m, pl_st, cv_st = odd_mixer(h, pos0, pool[i], conv[i], p['w_in_odd'][i], p['pool_w'][i],
                                        p['pool_scale'][i], p['conv_w'][i], p['conv_b'][i],
                                        p['conv_norm_g'][i], p['conv_norm_b'][i], p['w_out_odd'][i])
            n_pool.append(pl_st)
            n_conv.append(cv_st)
        x = x + m
        x = x + cross_attn(rmsnorm(x, p['norm_cross'][l]), mem_k[l], mem_v[l], p['xattn_q'][l], p['xattn_o'][l])
        h = rmsnorm(x, p['norm_ffn'][l])
        if l % 2 == 0:
            x = x + swiglu(h, p['ffn_w1'][i], p['ffn_w3'][i], p['ffn_w2'][i])
        else:
            x = x + moe(h, p['router_w'][i], p['router_b'][i], p['moe_w1'][i], p['moe_w3'][i], p['moe_w2'][i])
    y = rmsnorm(x, p['norm_final'])
    return (y, jnp.stack(n_ssm), jnp.stack(n_ssm_conv), jnp.stack(n_ret), jnp.stack(n_pool), jnp.stack(n_conv))


def setup_inputs(seed: int = 0) -> dict:
    key = jax.random.key(seed)
    keys = iter(jax.random.split(key, 64))

    def nrm(shape, scale=1.0):
        return jax.random.normal(next(keys), shape, F32) * scale

    def dense(shape, fan_in):
        return nrm(shape, fan_in ** -0.5)

    def gain(shape):
        return 1.0 + nrm(shape, 0.1)

    dt0 = jnp.exp(jax.random.uniform(next(keys), (N_EVEN, SSD_HEADS), F32) * (np.log(0.1) - np.log(0.001)) + np.log(0.001))
    return {
        'x_prompt': nrm((BATCH, SEQ, D_MODEL)),
        'x_sample': nrm((DEC_BATCH, DEC_SEQ, D_MODEL)),
        'state_ssm': nrm((N_EVEN, DEC_BATCH, SSD_HEADS, SSD_STATE, SSD_HEAD_DIM)),
        'state_ssm_conv': nrm((N_EVEN, DEC_BATCH, SSD_CONV - 1, SSD_CONV_CH)),
        'state_ret': nrm((N_EVEN, DEC_BATCH, RET_HEADS, RET_QK_DIM, RET_V_DIM)),
        'state_pool': nrm((N_ODD, DEC_BATCH, POOL_STATE, POOL_WIDTH)),
        'state_conv': nrm((N_ODD, DEC_BATCH, CONV_K - 1, CONV_WIDTH)),
        'cache_mem_k': nrm((DEPTH, DEC_BATCH, MEM_LEN, X_HEADS, X_HEAD_DIM)),
        'cache_mem_v': nrm((DEPTH, DEC_BATCH, MEM_LEN, X_HEADS, X_HEAD_DIM)),
        'mem_prompt': nrm((BATCH, MEM_LEN, D_MODEL)),
        'norm_mix': gain((DEPTH, D_MODEL)),
        'norm_cross': gain((DEPTH, D_MODEL)),
        'norm_ffn': gain((DEPTH, D_MODEL)),
        'norm_mem': gain((DEPTH, D_MODEL)),
        'norm_final': gain((D_MODEL,)),
        'w_in_even': dense((N_EVEN, D_MODEL, EVEN_IN), D_MODEL),
        'ssd_conv_w': dense((N_EVEN, SSD_CONV, SSD_CONV_CH), SSD_CONV),
        'ssd_conv_b': nrm((N_EVEN, SSD_CONV_CH), 0.02),
        'ssd_dt_bias': dt0 + jnp.log(-jnp.expm1(-dt0)),
        'ssd_a_log': jnp.log(jax.random.uniform(next(keys), (N_EVEN, SSD_HEADS), F32, 1.0, 16.0)),
        'ssd_d': gain((N_EVEN, SSD_HEADS)),
        'ssd_norm': gain((N_EVEN, SSD_D_INNER)),
        'w_out_even': dense((N_EVEN, EVEN_MIX, D_MODEL), EVEN_MIX),
        'ffn_w1': dense((N_EVEN, D_MODEL, D_FF), D_MODEL),
        'ffn_w3': dense((N_EVEN, D_MODEL, D_FF), D_MODEL),
        'ffn_w2': dense((N_EVEN, D_FF, D_MODEL), D_FF),
        'w_in_odd': dense((N_ODD, D_MODEL, ODD_IN), D_MODEL),
        'pool_w': dense((N_ODD, POOL_GROUPS, POOL_GROUP_DIM, POOL_GROUP_DIM), POOL_GROUP_DIM),
        'pool_scale': gain((N_ODD, POOL_WIDTH)),
        'conv_w': dense((N_ODD, CONV_K, CONV_WIDTH), CONV_K),
        'conv_b': nrm((N_ODD, CONV_WIDTH), 0.02),
        'conv_norm_g': gain((N_ODD, CONV_WIDTH)),
        'conv_norm_b': nrm((N_ODD, CONV_WIDTH), 0.02),
        'w_out_odd': dense((N_ODD, ODD_MIX, D_MODEL), ODD_MIX),
        'router_w': dense((N_ODD, D_MODEL, N_EXPERTS), D_MODEL),
        'router_b': nrm((N_ODD, N_EXPERTS), 0.01),
        'moe_w1': dense((N_ODD, N_EXPERTS, D_MODEL, D_FF_EXPERT), D_MODEL),
        'moe_w3': dense((N_ODD, N_EXPERTS, D_MODEL, D_FF_EXPERT), D_MODEL),
        'moe_w2': dense((N_ODD, N_EXPERTS, D_FF_EXPERT, D_MODEL), D_FF_EXPERT),
        'xattn_q': dense((DEPTH, D_MODEL, X_HEADS * X_HEAD_DIM), D_MODEL),
        'xattn_k': dense((DEPTH, D_MODEL, X_HEADS * X_HEAD_DIM), D_MODEL),
        'xattn_v': dense((DEPTH, D_MODEL, X_HEADS * X_HEAD_DIM), D_MODEL),
        'xattn_o': dense((DEPTH, X_HEADS * X_HEAD_DIM, D_MODEL), X_HEADS * X_HEAD_DIM),
    }


def reference(x_prompt, x_sample, state_ssm, state_ssm_conv, state_ret, state_pool, state_conv,
              cache_mem_k, cache_mem_v, mem_prompt,
              norm_mix, norm_cross, norm_ffn, norm_mem, norm_final,
              w_in_even, ssd_conv_w, ssd_conv_b, ssd_dt_bias, ssd_a_log, ssd_d, ssd_norm, w_out_even,
              ffn_w1, ffn_w3, ffn_w2,
              w_in_odd, pool_w, pool_scale, conv_w, conv_b, conv_norm_g, conv_norm_b, w_out_odd,
              router_w, router_b, moe_w1, moe_w3, moe_w2,
              xattn_q, xattn_k, xattn_v, xattn_o):
    p = dict(norm_mix=norm_mix, norm_cross=norm_cross, norm_ffn=norm_ffn, norm_final=norm_final,
             w_in_even=w_in_even, ssd_conv_w=ssd_conv_w, ssd_conv_b=ssd_conv_b, ssd_dt_bias=ssd_dt_bias,
             ssd_a_log=ssd_a_log, ssd_d=ssd_d, ssd_norm=ssd_norm, w_out_even=w_out_even,
             ffn_w1=ffn_w1, ffn_w3=ffn_w3, ffn_w2=ffn_w2,
             w_in_odd=w_in_odd, pool_w=pool_w, pool_scale=pool_scale, conv_w=conv_w, conv_b=conv_b,
             conv_norm_g=conv_norm_g, conv_norm_b=conv_norm_b, w_out_odd=w_out_odd,
             router_w=router_w, router_b=router_b, moe_w1=moe_w1, moe_w3=moe_w3, moe_w2=moe_w2,
             xattn_q=xattn_q, xattn_o=xattn_o)
    kv = [mem_kv(mem_prompt, norm_mem[l], xattn_k[l], xattn_v[l]) for l in range(DEPTH)]
    mem_k_prompt = jnp.stack([a for a, _ in kv])
    mem_v_prompt = jnp.stack([b for _, b in kv])
    dt_p = x_prompt.dtype
    y_prompt, ssm_p, ssm_conv_p, ret_p, pool_p, conv_p = trunk(
        x_prompt, 0, mem_k_prompt, mem_v_prompt,
        jnp.zeros((N_EVEN, BATCH, SSD_HEADS, SSD_STATE, SSD_HEAD_DIM), dt_p),
        jnp.zeros((N_EVEN, BATCH, SSD_CONV - 1, SSD_CONV_CH), dt_p),
        jnp.zeros((N_EVEN, BATCH, RET_HEADS, RET_QK_DIM, RET_V_DIM), dt_p),
        jnp.zeros((N_ODD, BATCH, POOL_STATE, POOL_WIDTH), dt_p),
        jnp.zeros((N_ODD, BATCH, CONV_K - 1, CONV_WIDTH), dt_p), p)
    y_sample, ssm_s, ssm_conv_s, ret_s, pool_s, conv_s = trunk(
        x_sample, PAST_LEN, cache_mem_k, cache_mem_v,
        state_ssm, state_ssm_conv, state_ret, state_pool, state_conv, p)
    return (y_prompt, y_sample, mem_k_prompt, mem_v_prompt,
            ssm_p, ssm_conv_p, ret_p, pool_p, conv_p,
            ssm_s, ssm_conv_s, ret_s, pool_s, conv_s)
```

```python
import functools

import jax
import jax.numpy as jnp
import numpy as np
from jax import lax
from jax.experimental import pallas as pl
from jax.experimental.pallas import tpu as pltpu

F32 = jnp.float32
BF16 = jnp.bfloat16
HIGHEST = lax.Precision.HIGHEST

D_MODEL = 1024
DEPTH = 4
PAST_LEN = 16384
EPS = 1e-6
CHUNK = 128
LANES = 128

SSD_HEAD_DIM = 64
SSD_HEADS = 16
SSD_GROUPS = 2
SSD_STATE = 128
SSD_CONV = 4
SSD_CONV_CH = 1536
RET_HEADS = 8
RET_QK_DIM = 64
RET_V_DIM = 128
RET_QK_WIDTH = 512
RET_V_WIDTH = 1024
ROPE_BASE = 10000.0
EVEN_MIX = 2048

POOL_WIDTH = 512
POOL_WINDOWS = (2, 4, 8, 16)
POOL_GROUP_DIM = 128
POOL_STATE = 15
CONV_WIDTH = 512
CONV_K = 31
MEM_LEN = 256
X_HEADS = 4
X_HEAD_DIM = 256
D_FF = 2816
N_EXPERTS = 8
D_FF_EXPERT = 1792

E_Z, E_XBC, E_Q, E_K, E_V, E_G, E_DT, E_END = 0, 1024, 2560, 3072, 3584, 4608, 5632, 5760

VMEM_LIMIT = 56 * 1024 * 1024


def _cparams(*sem):
    return pltpu.CompilerParams(dimension_semantics=sem, vmem_limit_bytes=VMEM_LIMIT)


def _full(shape):
    n = len(shape)
    return pl.BlockSpec(shape, lambda *_: (0,) * n)


def _rms(x, g):
    return x * lax.rsqrt(jnp.mean(x * x, axis=-1, keepdims=True) + EPS) * g


def _rms_only(x):
    return x * lax.rsqrt(jnp.mean(x * x, axis=-1, keepdims=True) + EPS)


def _silu(x):
    return x * jax.nn.sigmoid(x)


def _softplus(x):
    return jnp.maximum(x, 0.0) + jnp.log1p(jnp.exp(-jnp.abs(x)))


def _bdot(a, b):
    return jnp.dot(a.astype(BF16), b.astype(BF16), preferred_element_type=F32)


def _bdot_t(a, b):
    return lax.dot_general(a.astype(BF16), b.astype(BF16), (((1,), (1,)), ((), ())),
                           preferred_element_type=F32)


def _pair_cols(a, p, lo):
    return jnp.where(lo, a[:, 2 * p:2 * p + 1], a[:, 2 * p + 1:2 * p + 2])


def _rotary(x, cos_f, sin_s, lane):
    parts = []
    first_half = (lane % 64) < 32
    for c in range(x.shape[1] // LANES):
        xc = x[:, c * LANES:(c + 1) * LANES]
        parts.append(jnp.where(first_half, pltpu.roll(xc, 96, 1), pltpu.roll(xc, 32, 1)))
    return x * cos_f + jnp.concatenate(parts, axis=1) * sin_s


def _even_prompt_kernel(x_ref, g_ref, win_ref, cw_ref, cb_ref, dtb_ref, alog_ref, dsk_ref, gn_ref,
                        cos_ref, sin_ref, rd_ref, rea_ref, rwe_ref, rgt_ref, wout_ref,
                        xo_ref, nconv_ref, nssm_ref, nret_ref,
                        xbuf, sstate, rstate, *, T):
    t = pl.program_id(1)
    nt = pl.num_programs(1)

    @pl.when(t == 0)
    def _():
        xbuf[0:8, :] = jnp.zeros((8, SSD_CONV_CH), F32)
        sstate[...] = jnp.zeros_like(sstate)
        rstate[...] = jnp.zeros_like(rstate)

    x = x_ref[0]
    hb = _rms(x, g_ref[...]).astype(BF16)

    def proj(a, b):
        return jnp.dot(hb, win_ref[:, a:b], preferred_element_type=F32)

    xbuf[8:8 + T, :] = proj(E_XBC, E_Q)
    conv = cb_ref[...]
    for k in range(SSD_CONV):
        conv = conv + cw_ref[k:k + 1, :] * xbuf[5 + k:5 + k + T, :]
    tail = xbuf[T:T + 8, :]
    xbuf[0:8, :] = tail

    @pl.when(t == nt - 1)
    def _():
        nconv_ref[0] = tail[5:8, :]

    xc = _silu(conv)
    dt = _softplus(proj(E_DT, E_END) + dtb_ref[...])
    la = dt * (-jnp.exp(alog_ref[...]))

    lane = lax.broadcasted_iota(jnp.int32, (CHUNK, LANES), 1)
    row = lax.broadcasted_iota(jnp.int32, (CHUNK, LANES), 0)
    lo = lane < 64
    causal = row >= lane
    ltri = causal.astype(F32)

    q_all = _rotary(proj(E_Q, E_K), cos_ref[...], sin_ref[...], lane[:1])
    k_all = _rotary(proj(E_K, E_V), cos_ref[...], sin_ref[...], lane[:1]) * (RET_QK_DIM ** -0.5)
    v_all = proj(E_V, E_G)

    y_chunks, o_chunks = [], []
    for c in range(T // CHUNK):
        r0 = c * CHUNK
        rows = slice(r0, r0 + CHUNK)
        acum = jnp.dot(ltri, la[rows], precision=HIGHEST, preferred_element_type=F32)
        acum_t = acum.T
        tot = acum[CHUNK - 1:CHUNK, :]
        ea = jnp.exp(acum)
        wend = jnp.exp(tot - acum)
        etot = jnp.exp(tot)
        dt_c = dt[rows]
        y_pairs = []
        for g in range(SSD_GROUPS):
            b_g = xc[rows, 1024 + g * 128:1024 + (g + 1) * 128]
            c_g = xc[rows, 1280 + g * 128:1280 + (g + 1) * 128]
            gram = _bdot_t(c_g, b_g)
            y_inter = _bdot(c_g, sstate[g])
            xw, et = [], []
            for pp in range(4):
                p = g * 4 + pp
                xs_p = xc[rows, p * LANES:(p + 1) * LANES]
                xdt = xs_p * _pair_cols(dt_c, p, lo)
                s_pair = []
                for h in (2 * p, 2 * p + 1):
                    diff = acum[:, h:h + 1] - acum_t[h:h + 1, :]
                    dec = jnp.where(causal, jnp.exp(jnp.minimum(diff, 0.0)), 0.0)
                    s_pair.append((gram * dec).astype(BF16))
                rhs = jnp.concatenate([jnp.where(lo, xdt, 0.0), jnp.where(lo, 0.0, xdt)], axis=0)
                y_p = _bdot(jnp.concatenate(s_pair, axis=1), rhs)
                y_p = y_p + y_inter[:, pp * LANES:(pp + 1) * LANES] * _pair_cols(ea, p, lo)
                y_p = y_p + dsk_ref[:, p * LANES:(p + 1) * LANES] * xs_p
                y_pairs.append(y_p)
                xw.append(xdt * _pair_cols(wend, p, lo))
                et.append(_pair_cols(etot, p, lo[:1]))
            kv = _bdot(b_g.T, jnp.concatenate(xw, axis=1))
            sstate[g] = sstate[g] * jnp.concatenate(et, axis=1) + kv
        y_chunks.append(jnp.concatenate(y_pairs, axis=1))
        o_heads = []
        for p in range(RET_HEADS // 2):
            q_p = q_all[rows, p * LANES:(p + 1) * LANES]
            k_p = k_all[rows, p * LANES:(p + 1) * LANES]
            st_p = rstate[p * LANES:(p + 1) * LANES, :]
            ktw = k_p.T * rwe_ref[p]
            for hh in range(2):
                h = 2 * p + hh
                q_h = jnp.where(lo, q_p, 0.0) if hh == 0 else jnp.where(lo, 0.0, q_p)
                v_h = v_all[rows, h * LANES:(h + 1) * LANES]
                sc = (_bdot_t(q_h, k_p) * rd_ref[h]).astype(BF16)
                o_h = _bdot(sc, v_h) + _bdot(q_h, st_p) * rea_ref[h]
                o_heads.append(_rms_only(o_h))
                r = slice(h * 64, (h + 1) * 64)
                rstate[r, :] = rstate[r, :] * rgt_ref[r, :] + _bdot(ktw[hh * 64:(hh + 1) * 64, :], v_h)
        o_chunks.append(jnp.concatenate(o_heads, axis=1))

    y = jnp.concatenate(y_chunks, axis=0) if len(y_chunks) > 1 else y_chunks[0]
    o = jnp.concatenate(o_chunks, axis=0) if len(o_chunks) > 1 else o_chunks[0]
    y = y * _silu(proj(E_Z, E_XBC))
    y = jnp.concatenate([_rms_only(y[:, :512]), _rms_only(y[:, 512:])], axis=1) * gn_ref[...]
    o = o * _silu(proj(E_G, E_DT))
    mixed = jnp.concatenate([y, o], axis=1).astype(BF16)
    xo_ref[0] = x + jnp.dot(mixed, wout_ref[...], preferred_element_type=F32)

    @pl.when(t == nt - 1)
    def _():
        nssm_ref[0] = sstate[...]
        nret_ref[0] = rstate[...]


def _even_weights(w_in, i):
    w = w_in[i]
    cols = [w[:, 0:1024], w[:, 1024:2560], w[:, 2576:3088], w[:, 3088:3600], w[:, 3600:4624],
            w[:, 4624:5648], w[:, 2560:2576], jnp.zeros((D_MODEL, LANES - SSD_HEADS), w.dtype)]
    return jnp.concatenate(cols, axis=1).astype(BF16)


def _pad_lanes(v):
    return jnp.pad(v.astype(F32), (0, LANES - v.shape[0]))[None, :]


def _even_layer_params(p, i):
    return (_even_weights(p['w_in_even'], i), p['ssd_conv_w'][i], p['ssd_conv_b'][i][None],
            _pad_lanes(p['ssd_dt_bias'][i]), _pad_lanes(p['ssd_a_log'][i]),
            jnp.repeat(p['ssd_d'][i].astype(F32), SSD_HEAD_DIM)[None], p['ssd_norm'][i][None],
            p['w_out_even'][i].astype(BF16))


def _rope_tables(pos):
    half = RET_QK_DIM // 2
    inv = ROPE_BASE ** (-jnp.arange(half, dtype=F32) / half)
    ang = pos[:, None] * inv[None, :]
    cos, sin = jnp.cos(ang), jnp.sin(ang)
    cos_f = jnp.tile(jnp.concatenate([cos, cos], axis=1), (1, RET_HEADS))
    sin_s = jnp.tile(jnp.concatenate([-sin, sin], axis=1), (1, RET_HEADS))
    return cos_f, sin_s


def _ret_consts():
    lg = jnp.log1p(-jnp.exp2(-5.0 - jnp.arange(RET_HEADS, dtype=F32)))
    acum = jnp.cumsum(jnp.broadcast_to(lg[:, None], (RET_HEADS, CHUNK)), axis=1)
    idx = jnp.arange(CHUNK)
    diff = acum[:, :, None] - acum[:, None, :]
    rd = jnp.where((idx[:, None] >= idx[None, :])[None], jnp.exp(jnp.minimum(diff, 0.0)), 0.0)
    rea = jnp.broadcast_to(jnp.exp(acum)[:, :, None], (RET_HEADS, CHUNK, RET_V_DIM))
    total = acum[:, -1]
    we = jnp.exp(total[:, None] - acum)
    rwe = jnp.broadcast_to(we[:, None, :], (RET_HEADS, RET_QK_DIM, CHUNK)).reshape(RET_HEADS // 2, 128, CHUNK)
    rgt = jnp.broadcast_to(jnp.exp(total)[:, None, None], (RET_HEADS, RET_QK_DIM, RET_V_DIM)).reshape(512, RET_V_DIM)
    return rd, rea, rwe, rgt, lg


def _even_prompt(x, g, win, cw, cb, dtb, alog, dsk, gn, wout, cos_f, sin_s, rconsts, T=256):
    B, L, D = x.shape
    rd, rea, rwe, rgt, _ = rconsts
    grid = (B, L // T)
    in_specs = [
        pl.BlockSpec((1, T, D), lambda b, t: (b, t, 0)),
        _full((1, D)), _full(win.shape), _full(cw.shape), _full(cb.shape), _full(dtb.shape),
        _full(alog.shape), _full(dsk.shape), _full(gn.shape),
        pl.BlockSpec((T, RET_QK_WIDTH), lambda b, t: (t, 0)),
        pl.BlockSpec((T, RET_QK_WIDTH), lambda b, t: (t, 0)),
        _full(rd.shape), _full(rea.shape), _full(rwe.shape), _full(rgt.shape), _full(wout.shape),
    ]
    out_shape = [
        jax.ShapeDtypeStruct((B, L, D), F32),
        jax.ShapeDtypeStruct((B, SSD_CONV - 1, SSD_CONV_CH), F32),
        jax.ShapeDtypeStruct((B, SSD_GROUPS, SSD_STATE, 512), F32),
        jax.ShapeDtypeStruct((B, 512, RET_V_DIM), F32),
    ]
    out_specs = [
        pl.BlockSpec((1, T, D), lambda b, t: (b, t, 0)),
        pl.BlockSpec((1, SSD_CONV - 1, SSD_CONV_CH), lambda b, t: (b, 0, 0)),
        pl.BlockSpec((1, SSD_GROUPS, SSD_STATE, 512), lambda b, t: (b, 0, 0, 0)),
        pl.BlockSpec((1, 512, RET_V_DIM), lambda b, t: (b, 0, 0)),
    ]
    scratch = [pltpu.VMEM((T + 8, SSD_CONV_CH), F32), pltpu.VMEM((SSD_GROUPS, SSD_STATE, 512), F32),
               pltpu.VMEM((512, RET_V_DIM), F32)]
    xo, nconv, nssm, nret = pl.pallas_call(
        functools.partial(_even_prompt_kernel, T=T), grid=grid, in_specs=in_specs, out_specs=out_specs,
        out_shape=out_shape, scratch_shapes=scratch, compiler_params=_cparams("parallel", "arbitrary"),
        name="even_prompt",
    )(x, g, win, cw, cb, dtb, alog, dsk, gn, cos_f, sin_s, rd, rea, rwe, rgt, wout)
    nssm = nssm.reshape(B, SSD_GROUPS, SSD_STATE, 8, SSD_HEAD_DIM).transpose(0, 1, 3, 2, 4)
    nssm = nssm.reshape(B, SSD_HEADS, SSD_STATE, SSD_HEAD_DIM)
    nret = nret.reshape(B, RET_HEADS, RET_QK_DIM, RET_V_DIM)
    return xo, nconv, nssm, nret


SSM_ROW = SSD_STATE * SSD_HEAD_DIM
RET_ROW = RET_QK_DIM * RET_V_DIM
SB = 8


def _even_sample_kernel(x_ref, g_ref, win_ref, cw_ref, cb_ref, dtb_ref, alog_ref, dsk_ref, gn_ref,
                        cos_ref, sin_ref, gam_ref, wout_ref, cst_ref, sst_ref, rst_ref,
                        xo_ref, nconv_ref, nsst_ref, nrst_ref,
                        hb_s, xc_s, dt_s, e_s, q_s, k_s, v_s, y_s, o_s):
    i = pl.program_id(0)
    n = pl.num_programs(0)
    lane = lax.broadcasted_iota(jnp.int32, (1, LANES), 1)
    lo = lane < 64

    @pl.when(i == 0)
    def _():
        hb = _rms(x_ref[...], g_ref[...]).astype(BF16)
        hb_s[...] = hb

        def proj(a, b):
            return jnp.dot(hb, win_ref[:, a:b], preferred_element_type=F32)

        xbc = proj(E_XBC, E_Q)
        w = SSD_CONV_CH
        conv = cb_ref[...] + cw_ref[3:4, :] * xbc
        for k in range(SSD_CONV - 1):
            conv = conv + cw_ref[k:k + 1, :] * cst_ref[:, k * w:(k + 1) * w]
        nconv_ref[:, 0:2 * w] = cst_ref[:, w:3 * w]
        nconv_ref[:, 2 * w:3 * w] = xbc
        xc_s[...] = _silu(conv)
        dt = _softplus(proj(E_DT, E_END) + dtb_ref[...])
        dt_s[...] = dt
        e_s[...] = jnp.exp(dt * (-jnp.exp(alog_ref[...])))
        q_s[...] = _rotary(proj(E_Q, E_K), cos_ref[...], sin_ref[...], lane)
        k_s[...] = _rotary(proj(E_K, E_V), cos_ref[...], sin_ref[...], lane) * (RET_QK_DIM ** -0.5)
        v_s[...] = proj(E_V, E_G)

    r = pl.ds(pl.multiple_of(i * SB, SB), SB)
    xc = xc_s[r, :]
    dt = dt_s[r, :]
    er = e_s[r, :]
    for g in range(SSD_GROUPS):
        b_g = xc[:, 1024 + g * 128:1024 + (g + 1) * 128]
        c_g = xc[:, 1280 + g * 128:1280 + (g + 1) * 128]
        cb_dot = jnp.sum(c_g * b_g, axis=-1, keepdims=True)
        xps, xreps, ehs = [], [], []
        for hh in range(8):
            h = g * 8 + hh
            p = h // 2
            if hh % 2 == 0:
                xp = xc[:, p * LANES:(p + 1) * LANES] * _pair_cols(dt, p, lo)
                xps.append(xp)
                xreps.append(jnp.where(lo, xp, pltpu.roll(xp, 64, 1)))
            else:
                xreps.append(jnp.where(lo, pltpu.roll(xps[-1], 64, 1), xps[-1]))
            ehs.append(er[:, h:h + 1])
        accs = [jnp.zeros((SB, LANES), F32) for _ in range(8)]
        for r2 in range(SSD_STATE // 2):
            b_p = jnp.where(lo, b_g[:, 2 * r2:2 * r2 + 1], b_g[:, 2 * r2 + 1:2 * r2 + 2])
            c_p = jnp.where(lo, c_g[:, 2 * r2:2 * r2 + 1], c_g[:, 2 * r2 + 1:2 * r2 + 2])
            for hh in range(8):
                col = (g * 8 + hh) * SSM_ROW + r2 * LANES
                blk = sst_ref[:, col:col + LANES]
                accs[hh] = accs[hh] + c_p * blk
                nsst_ref[:, col:col + LANES] = ehs[hh] * blk + b_p * xreps[hh]
        for pp in range(4):
            p = g * 4 + pp
            ya = accs[2 * pp] + pltpu.roll(accs[2 * pp], 64, 1)
            yb = accs[2 * pp + 1] + pltpu.roll(accs[2 * pp + 1], 64, 1)
            y_p = cb_dot * xps[pp] + _pair_cols(er, p, lo) * jnp.where(lo, ya, yb)
            y_p = y_p + dsk_ref[:, p * LANES:(p + 1) * LANES] * xc[:, p * LANES:(p + 1) * LANES]
            y_s[r, p * LANES:(p + 1) * LANES] = y_p
    q = q_s[r, :]
    k = k_s[r, :]
    v = v_s[r, :]
    for h in range(RET_HEADS):
        p = h // 2
        prod = q[:, p * LANES:(p + 1) * LANES] * k[:, p * LANES:(p + 1) * LANES]
        keep = lo if h % 2 == 0 else jnp.logical_not(lo)
        qk = jnp.sum(jnp.where(keep, prod, 0.0), axis=-1, keepdims=True)
        v_h = v[:, h * LANES:(h + 1) * LANES]
        gam = gam_ref[:, h:h + 1]
        acc = jnp.zeros((SB, LANES), F32)
        for kk in range(RET_QK_DIM):
            ln = h * RET_QK_DIM + kk
            col = h * RET_ROW + kk * LANES
            blk = rst_ref[:, col:col + LANES]
            acc = acc + q[:, ln:ln + 1] * blk
            nrst_ref[:, col:col + LANES] = gam * blk + k[:, ln:ln + 1] * v_h
        o_s[r, h * LANES:(h + 1) * LANES] = _rms_only(qk * v_h + gam * acc)

    @pl.when(i == n - 1)
    def _():
        hb = hb_s[...]
        y = y_s[...] * _silu(jnp.dot(hb, win_ref[:, E_Z:E_XBC], preferred_element_type=F32))
        y = jnp.concatenate([_rms_only(y[:, :512]), _rms_only(y[:, 512:])], axis=1) * gn_ref[...]
        o = o_s[...] * _silu(jnp.dot(hb, win_ref[:, E_G:E_DT], preferred_element_type=F32))
        mixed = jnp.concatenate([y, o], axis=1).astype(BF16)
        xo_ref[...] = x_ref[...] + jnp.dot(mixed, wout_ref[...], preferred_element_type=F32)


def _even_sample(x, g, win, cw, cb, dtb, alog, dsk, gn, wout, cos_f, sin_s, gam, conv_st, ssm_st, ret_st):
    S, D = x.shape
    cst = conv_st.reshape(S, (SSD_CONV - 1) * SSD_CONV_CH)
    sst = ssm_st.reshape(S, SSD_HEADS * SSM_ROW)
    rst = ret_st.reshape(S, RET_HEADS * RET_ROW)
    step = lambda w: pl.BlockSpec((SB, w), lambda i: (i, 0))
    in_specs = [_full(x.shape), _full(g.shape), _full(win.shape), _full(cw.shape), _full(cb.shape),
                _full(dtb.shape), _full(alog.shape), _full(dsk.shape), _full(gn.shape),
                _full(cos_f.shape), _full(sin_s.shape), _full(gam.shape), _full(wout.shape),
                _full(cst.shape), step(sst.shape[1]), step(rst.shape[1])]
    out_shape = [jax.ShapeDtypeStruct(a.shape, F32) for a in (x, cst, sst, rst)]
    out_specs = [_full(x.shape), _full(cst.shape), step(sst.shape[1]), step(rst.shape[1])]
    scratch = [pltpu.VMEM((S, D), BF16), pltpu.VMEM((S, SSD_CONV_CH), F32), pltpu.VMEM((S, LANES), F32),
               pltpu.VMEM((S, LANES), F32), pltpu.VMEM((S, RET_QK_WIDTH), F32), pltpu.VMEM((S, RET_QK_WIDTH), F32),
               pltpu.VMEM((S, RET_V_WIDTH), F32), pltpu.VMEM((S, D), F32), pltpu.VMEM((S, RET_V_WIDTH), F32)]
    xo, nconv, nsst, nrst = pl.pallas_call(
        _even_sample_kernel, grid=(S // SB,), in_specs=in_specs, out_specs=out_specs, out_shape=out_shape,
        scratch_shapes=scratch, compiler_params=_cparams("arbitrary"), name="even_sample",
    )(x, g, win, cw, cb, dtb, alog, dsk, gn, cos_f, sin_s, gam, wout, cst, sst, rst)
    return xo, nconv.reshape(conv_st.shape), nsst.reshape(ssm_st.shape), nrst.reshape(ret_st.shape)


def _layernorm_silu(c, g, b):
    mu = jnp.mean(c, axis=-1, keepdims=True)
    var = jnp.mean(jnp.square(c - mu), axis=-1, keepdims=True)
    return _silu((c - mu) * lax.rsqrt(var + EPS) * g + b)


def _pool_project(parts, pw_ref, ps_ref):
    proj = [_bdot(parts[gi], pw_ref[gi]) for gi in range(len(POOL_WINDOWS))]
    return jnp.concatenate(proj, axis=1) * ps_ref[...]


def _odd_prompt_kernel(x_ref, g_ref, win_ref, pw_ref, ps_ref, cw_ref, cb_ref, cng_ref, cnb_ref, wout_ref,
                       xo_ref, npool_ref, nconv_ref, ubuf, gbuf, *, T):
    t = pl.program_id(1)
    nt = pl.num_programs(1)

    @pl.when(t == 0)
    def _():
        ubuf[0:16, :] = jnp.zeros((16, POOL_WIDTH), F32)
        gbuf[0:32, :] = jnp.zeros((32, CONV_WIDTH), F32)

    x = x_ref[0]
    hb = _rms(x, g_ref[...]).astype(BF16)
    u = jnp.dot(hb, win_ref[:, 0:512], preferred_element_type=F32)
    ga = jnp.dot(hb, win_ref[:, 512:1024], preferred_element_type=F32)
    gb = jnp.dot(hb, win_ref[:, 1024:1536], preferred_element_type=F32)
    ubuf[16:16 + T, :] = u
    pos1 = (lax.broadcasted_iota(jnp.int32, (T, 1), 0) + t * T + 1).astype(F32)
    parts = []
    for gi, w in enumerate(POOL_WINDOWS):
        sl = slice(gi * POOL_GROUP_DIM, (gi + 1) * POOL_GROUP_DIM)
        s = u[:, sl]
        for k in range(1, w):
            s = s + ubuf[16 - k:16 - k + T, sl]
        parts.append(s / jnp.minimum(pos1, float(w)) - u[:, sl])
    pooled = _pool_project(parts, pw_ref, ps_ref)
    utail = ubuf[T:T + 16, :]
    ubuf[0:16, :] = utail
    glu = ga * jax.nn.sigmoid(gb)
    gbuf[32:32 + T, :] = glu
    c = cb_ref[...] + cw_ref[CONV_K - 1:CONV_K, :] * glu
    for k in range(CONV_K - 1):
        c = c + cw_ref[k:k + 1, :] * gbuf[2 + k:2 + k + T, :]
    gtail = gbuf[T:T + 32, :]
    gbuf[0:32, :] = gtail
    c = _layernorm_silu(c, cng_ref[...], cnb_ref[...])
    mixed = jnp.concatenate([pooled, c], axis=1).astype(BF16)
    xo_ref[0] = x + jnp.dot(mixed, wout_ref[...], preferred_element_type=F32)

    @pl.when(t == nt - 1)
    def _():
        npool_ref[0] = utail[1:16, :]
        nconv_ref[0] = gtail[2:32, :]


def _odd_layer_params(p, i):
    return (p['w_in_odd'][i].astype(BF16), p['pool_w'][i].astype(BF16), p['pool_scale'][i][None],
            p['conv_w'][i], p['conv_b'][i][None], p['conv_norm_g'][i][None], p['conv_norm_b'][i][None],
            p['w_out_odd'][i].astype(BF16))


def _odd_prompt(x, g, win, pw, ps, cw, cb, cng, cnb, wout, T=256):
    B, L, D = x.shape
    consts = (g, win, pw, ps, cw, cb, cng, cnb, wout)
    in_specs = [pl.BlockSpec((1, T, D), lambda b, t: (b, t, 0))] + [_full(a.shape) for a in consts]
    out_shape = [jax.ShapeDtypeStruct((B, L, D), F32), jax.ShapeDtypeStruct((B, POOL_STATE, POOL_WIDTH), F32),
                 jax.ShapeDtypeStruct((B, CONV_K - 1, CONV_WIDTH), F32)]
    out_specs = [pl.BlockSpec((1, T, D), lambda b, t: (b, t, 0)),
                 pl.BlockSpec((1, POOL_STATE, POOL_WIDTH), lambda b, t: (b, 0, 0)),
                 pl.BlockSpec((1, CONV_K - 1, CONV_WIDTH), lambda b, t: (b, 0, 0))]
    scratch = [pltpu.VMEM((T + 16, POOL_WIDTH), F32), pltpu.VMEM((T + 32, CONV_WIDTH), F32)]
    return pl.pallas_call(
        functools.partial(_odd_prompt_kernel, T=T), grid=(B, L // T), in_specs=in_specs, out_specs=out_specs,
        out_shape=out_shape, scratch_shapes=scratch, compiler_params=_cparams("parallel", "arbitrary"),
        name="odd_prompt",
    )(x, *consts)


def _odd_sample_kernel(x_ref, g_ref, win_ref, pw_ref, ps_ref, cw_ref, cb_ref, cng_ref, cnb_ref, wout_ref,
                       pst_ref, cst_ref, xo_ref, npool_ref, nconv_ref):
    x = x_ref[...]
    hb = _rms(x, g_ref[...]).astype(BF16)
    u = jnp.dot(hb, win_ref[:, 0:512], preferred_element_type=F32)
    ga = jnp.dot(hb, win_ref[:, 512:1024], preferred_element_type=F32)
    gb = jnp.dot(hb, win_ref[:, 1024:1536], preferred_element_type=F32)
    pw_, cw_ = POOL_WIDTH, CONV_WIDTH
    parts = []
    for gi, w in enumerate(POOL_WINDOWS):
        lo_, hi_ = gi * POOL_GROUP_DIM, (gi + 1) * POOL_GROUP_DIM
        s = u[:, lo_:hi_]
        for k in range(1, w):
            j = POOL_STATE - k
            s = s + pst_ref[:, j * pw_ + lo_:j * pw_ + hi_]
        parts.append(s / float(w) - u[:, lo_:hi_])
    pooled = _pool_project(parts, pw_ref, ps_ref)
    npool_ref[:, 0:(POOL_STATE - 1) * pw_] = pst_ref[:, pw_:POOL_STATE * pw_]
    npool_ref[:, (POOL_STATE - 1) * pw_:POOL_STATE * pw_] = u
    glu = ga * jax.nn.sigmoid(gb)
    c = cb_ref[...] + cw_ref[CONV_K - 1:CONV_K, :] * glu
    for k in range(CONV_K - 1):
        c = c + cw_ref[k:k + 1, :] * cst_ref[:, k * cw_:(k + 1) * cw_]
    nconv_ref[:, 0:(CONV_K - 2) * cw_] = cst_ref[:, cw_:(CONV_K - 1) * cw_]
    nconv_ref[:, (CONV_K - 2) * cw_:(CONV_K - 1) * cw_] = glu
    c = _layernorm_silu(c, cng_ref[...], cnb_ref[...])
    mixed = jnp.concatenate([pooled, c], axis=1).astype(BF16)
    xo_ref[...] = x + jnp.dot(mixed, wout_ref[...], preferred_element_type=F32)


def _odd_sample(x, g, win, pw, ps, cw, cb, cng, cnb, wout, pool_st, conv_st):
    S, D = x.shape
    pst = pool_st.reshape(S, POOL_STATE * POOL_WIDTH)
    cst = conv_st.reshape(S, (CONV_K - 1) * CONV_WIDTH)
    args = (x, g, win, pw, ps, cw, cb, cng, cnb, wout, pst, cst)
    outs = (x, pst, cst)
    xo, npool, nconv = pl.pallas_call(
        _odd_sample_kernel, grid=(1,), in_specs=[_full(a.shape) for a in args],
        out_specs=[_full(a.shape) for a in outs], out_shape=[jax.ShapeDtypeStruct(a.shape, F32) for a in outs],
        compiler_params=_cparams("arbitrary"), name="odd_sample",
    )(*args)
    return xo, npool.reshape(pool_st.shape), nconv.reshape(conv_st.shape)


def _mem_kv_kernel(m_ref, g_ref, w_ref, k_ref, v_ref):
    hb = _rms(m_ref[...], g_ref[0]).astype(BF16)
    kv = jnp.dot(hb, w_ref[0], preferred_element_type=F32)
    k_ref[0] = kv[:, :D_MODEL]
    v_ref[0] = kv[:, D_MODEL:]


def _mem_kv(mem, norm_mem, wkv, tm=512):
    M, D = mem.shape
    out = jax.ShapeDtypeStruct((DEPTH, M, D), F32)
    return pl.pallas_call(
        _mem_kv_kernel, grid=(DEPTH, M // tm),
        in_specs=[pl.BlockSpec((tm, D), lambda l, m: (m, 0)), pl.BlockSpec((1, 1, D), lambda l, m: (l, 0, 0)),
                  pl.BlockSpec((1, D, 2 * D), lambda l, m: (l, 0, 0))],
        out_specs=[pl.BlockSpec((1, tm, D), lambda l, m: (l, m, 0))] * 2, out_shape=[out, out],
        compiler_params=_cparams("parallel", "parallel"), name="mem_kv",
    )(mem, norm_mem[:, None, :], wkv)


def _xattn_prompt_kernel(x_ref, g_ref, wq_ref, k_ref, v_ref, wo_ref, xo_ref):
    x = x_ref[0]
    hb = _rms(x, g_ref[...]).astype(BF16)
    q = jnp.dot(hb, wq_ref[...], preferred_element_type=F32) * (X_HEAD_DIM ** -0.5)
    outs = []
    for h in range(X_HEADS):
        sl = slice(h * X_HEAD_DIM, (h + 1) * X_HEAD_DIM)
        s = _bdot_t(q[:, sl], k_ref[0, :, sl])
        p = jnp.exp(s - jnp.max(s, axis=-1, keepdims=True))
        outs.append(_bdot(p, v_ref[0, :, sl]) / jnp.sum(p, axis=-1, keepdims=True))
    o = jnp.concatenate(outs, axis=1).astype(BF16)
    xo_ref[0] = x + jnp.dot(o, wo_ref[...], preferred_element_type=F32)


def _xattn_prompt(x, g, wq, mk, mv, wo, T=512):
    B, L, D = x.shape
    xspec = pl.BlockSpec((1, T, D), lambda b, t: (b, t, 0))
    mspec = pl.BlockSpec((1, MEM_LEN, D), lambda b, t: (b, 0, 0))
    return pl.pallas_call(
        _xattn_prompt_kernel, grid=(B, L // T),
        in_specs=[xspec, _full(g.shape), _full(wq.shape), mspec, mspec, _full(wo.shape)],
        out_specs=xspec, out_shape=jax.ShapeDtypeStruct(x.shape, F32),
        compiler_params=_cparams("parallel", "parallel"), name="xattn_prompt",
    )(x, g, wq, mk, mv, wo)


XB = 4


def _xattn_sample_kernel(x_ref, g_ref, wq_ref, k_ref, v_ref, wo_ref, xo_ref, q_s, o_s):
    i = pl.program_id(0)
    n = pl.num_programs(0)

    @pl.when(i == 0)
    def _():
        hb = _rms(x_ref[...], g_ref[...]).astype(BF16)
        q_s[...] = jnp.dot(hb, wq_ref[...], preferred_element_type=F32) * (X_HEAD_DIM ** -0.5)

    for j in range(XB):
        r = pl.ds(i * XB + j, 1)
        prod = k_ref[j] * q_s[r, :]
        outs = []
        for h in range(X_HEADS):
            sl = slice(h * X_HEAD_DIM, (h + 1) * X_HEAD_DIM)
            s = jnp.sum(prod[:, sl], axis=-1, keepdims=True)
            p = jnp.exp(s - jnp.max(s, axis=0, keepdims=True))
            num = jnp.sum(p * v_ref[j, :, sl], axis=0, keepdims=True)
            outs.append(num / jnp.sum(p, axis=0, keepdims=True))
        o_s[r, :] = jnp.concatenate(outs, axis=1)

    @pl.when(i == n - 1)
    def _():
        xo_ref[...] = x_ref[...] + jnp.dot(o_s[...].astype(BF16), wo_ref[...], preferred_element_type=F32)


def _xattn_sample(x, g, wq, ck, cv, wo):
    S, D = x.shape
    cspec = pl.BlockSpec((XB, MEM_LEN, D), lambda i: (i, 0, 0))
    return pl.pallas_call(
        _xattn_sample_kernel, grid=(S // XB,),
        in_specs=[_full(x.shape), _full(g.shape), _full(wq.shape), cspec, cspec, _full(wo.shape)],
        out_specs=_full(x.shape), out_shape=jax.ShapeDtypeStruct(x.shape, F32),
        scratch_shapes=[pltpu.VMEM((S, D), F32), pltpu.VMEM((S, D), F32)],
        compiler_params=_cparams("arbitrary"), name="xattn_sample",
    )(x, g, wq, ck, cv, wo)


def _ffn_kernel(x_ref, g_ref, w1_ref, w3_ref, w2_ref, xo_ref):
    x = x_ref[...]
    hb = _rms(x, g_ref[...]).astype(BF16)
    a = jnp.dot(hb, w1_ref[...], preferred_element_type=F32)
    b = jnp.dot(hb, w3_ref[...], preferred_element_type=F32)
    xo_ref[...] = x + jnp.dot((_silu(a) * b).astype(BF16), w2_ref[...], preferred_element_type=F32)


def _ffn(x, g, w1, w3, w2, tm):
    M, D = x.shape
    xspec = pl.BlockSpec((tm, D), lambda m: (m, 0))
    return pl.pallas_call(
        _ffn_kernel, grid=(M // tm,),
        in_specs=[xspec, _full(g.shape), _full(w1.shape), _full(w3.shape), _full(w2.shape)],
        out_specs=xspec, out_shape=jax.ShapeDtypeStruct(x.shape, F32),
        compiler_params=_cparams("parallel"), name="ffn",
    )(x, g, w1, w3, w2)


def _moe_kernel(x_ref, g_ref, rw_ref, rb_ref, w1_ref, w3_ref, w2_ref, xo_ref, hb_s, comb_s, acc_s):
    e = pl.program_id(1)
    lane = lax.broadcasted_iota(jnp.int32, (1, LANES), 1)

    @pl.when(e == 0)
    def _():
        h = _rms(x_ref[...], g_ref[...])
        hb_s[...] = h.astype(BF16)
        logits = jnp.dot(h, rw_ref[...], precision=HIGHEST, preferred_element_type=F32) + rb_ref[...]
        m1 = jnp.max(logits, axis=-1, keepdims=True)
        i1 = jnp.min(jnp.where(logits == m1, lane, LANES), axis=-1, keepdims=True)
        rest = jnp.where(lane == i1, -jnp.inf, logits)
        m2 = jnp.max(rest, axis=-1, keepdims=True)
        i2 = jnp.min(jnp.where(rest == m2, lane, LANES), axis=-1, keepdims=True)
        e2 = jnp.exp(m2 - m1)
        comb_s[...] = jnp.where(lane == i1, 1.0 / (1.0 + e2), 0.0) + jnp.where(lane == i2, e2 / (1.0 + e2), 0.0)
        acc_s[...] = jnp.zeros_like(acc_s)

    hb = hb_s[...]
    a = jnp.dot(hb, w1_ref[0], preferred_element_type=F32)
    b = jnp.dot(hb, w3_ref[0], preferred_element_type=F32)
    y = jnp.dot((_silu(a) * b).astype(BF16), w2_ref[0], preferred_element_type=F32)
    gate = jnp.sum(jnp.where(lane == e, comb_s[...], 0.0), axis=-1, keepdims=True)
    acc_s[...] += gate * y

    @pl.when(e == pl.num_programs(1) - 1)
    def _():
        xo_ref[...] = x_ref[...] + acc_s[...]


def _moe_layer_params(p, i):
    rw = jnp.pad(p['router_w'][i].astype(F32), ((0, 0), (0, LANES - N_EXPERTS)))
    rb = jnp.concatenate([p['router_b'][i].astype(F32), jnp.full((LANES - N_EXPERTS,), -1e30, F32)])[None]
    return rw, rb, p['moe_w1'][i].astype(BF16), p['moe_w3'][i].astype(BF16), p['moe_w2'][i].astype(BF16)


def _moe(x, g, rw, rb, w1, w3, w2, tm):
    M, D = x.shape
    xspec = pl.BlockSpec((tm, D), lambda m, e: (m, 0))
    return pl.pallas_call(
        _moe_kernel, grid=(M // tm, N_EXPERTS),
        in_specs=[xspec, _full(g.shape), _full(rw.shape), _full(rb.shape),
                  pl.BlockSpec((1, D, D_FF_EXPERT), lambda m, e: (e, 0, 0)),
                  pl.BlockSpec((1, D, D_FF_EXPERT), lambda m, e: (e, 0, 0)),
                  pl.BlockSpec((1, D_FF_EXPERT, D), lambda m, e: (e, 0, 0))],
        out_specs=xspec, out_shape=jax.ShapeDtypeStruct(x.shape, F32),
        scratch_shapes=[pltpu.VMEM((tm, D), BF16), pltpu.VMEM((tm, LANES), F32), pltpu.VMEM((tm, D), F32)],
        compiler_params=_cparams("parallel", "arbitrary"), name="moe",
    )(x, g, rw, rb, w1, w3, w2)


def _final_norm_kernel(x_ref, g_ref, o_ref):
    o_ref[...] = _rms(x_ref[...], g_ref[...])


def _final_norm(x, g, tm):
    M, D = x.shape
    spec = pl.BlockSpec((tm, D), lambda m: (m, 0))
    return pl.pallas_call(
        _final_norm_kernel, grid=(M // tm,), in_specs=[spec, _full(g.shape)], out_specs=spec,
        out_shape=jax.ShapeDtypeStruct(x.shape, F32), compiler_params=_cparams("parallel"), name="final_norm",
    )(x, g)


def kernel(x_prompt, x_sample, state_ssm, state_ssm_conv, state_ret, state_pool, state_conv, cache_mem_k, cache_mem_v, mem_prompt, norm_mix, norm_cross, norm_ffn, norm_mem, norm_final, w_in_even, ssd_conv_w, ssd_conv_b, ssd_dt_bias, ssd_a_log, ssd_d, ssd_norm, w_out_even, ffn_w1, ffn_w3, ffn_w2, w_in_odd, pool_w, pool_scale, conv_w, conv_b, conv_norm_g, conv_norm_b, w_out_odd, router_w, router_b, moe_w1, moe_w3, moe_w2, xattn_q, xattn_k, xattn_v, xattn_o):
    p = dict(norm_mix=norm_mix, norm_cross=norm_cross, norm_ffn=norm_ffn, norm_mem=norm_mem,
             w_in_even=w_in_even, ssd_conv_w=ssd_conv_w, ssd_conv_b=ssd_conv_b, ssd_dt_bias=ssd_dt_bias,
             ssd_a_log=ssd_a_log, ssd_d=ssd_d, ssd_norm=ssd_norm, w_out_even=w_out_even,
             ffn_w1=ffn_w1, ffn_w3=ffn_w3, ffn_w2=ffn_w2,
             w_in_odd=w_in_odd, pool_w=pool_w, pool_scale=pool_scale, conv_w=conv_w, conv_b=conv_b,
             conv_norm_g=conv_norm_g, conv_norm_b=conv_norm_b, w_out_odd=w_out_odd,
             router_w=router_w, router_b=router_b, moe_w1=moe_w1, moe_w3=moe_w3, moe_w2=moe_w2,
             xattn_q=xattn_q, xattn_k=xattn_k, xattn_v=xattn_v, xattn_o=xattn_o)
    B, L, D = x_prompt.shape
    S = x_sample.shape[0]
    rconsts = _ret_consts()
    cos_p, sin_p = _rope_tables(jnp.arange(L, dtype=F32))
    cos_s, sin_s = _rope_tables(PAST_LEN + jnp.arange(1, dtype=F32))
    gam = _pad_lanes(jnp.exp(rconsts[4]))

    mem_k, mem_v = _mem_kv(mem_prompt.reshape(B * MEM_LEN, D), norm_mem,
                           jnp.concatenate([xattn_k, xattn_v], axis=2).astype(BF16))
    mem_k = mem_k.reshape(DEPTH, B, MEM_LEN, D)
    mem_v = mem_v.reshape(DEPTH, B, MEM_LEN, D)
    cache_k = cache_mem_k.reshape(DEPTH, S, MEM_LEN, D)
    cache_v = cache_mem_v.reshape(DEPTH, S, MEM_LEN, D)

    xp = x_prompt
    xs = x_sample.reshape(S, D)
    outs_p = dict(ssm=[], ssm_conv=[], ret=[], pool=[], conv=[])
    outs_s = dict(ssm=[], ssm_conv=[], ret=[], pool=[], conv=[])
    for l in range(DEPTH):
        i = l // 2
        g_mix = norm_mix[l][None]
        if l % 2 == 0:
            ew = _even_layer_params(p, i)
            xp, c_st, s_st, r_st = _even_prompt(xp, g_mix, *ew, cos_p, sin_p, rconsts)
            outs_p['ssm_conv'].append(c_st); outs_p['ssm'].append(s_st); outs_p['ret'].append(r_st)
            xs, c_st, s_st, r_st = _even_sample(xs, g_mix, *ew, cos_s, sin_s, gam,
                                                state_ssm_conv[i], state_ssm[i], state_ret[i])
            outs_s['ssm_conv'].append(c_st); outs_s['ssm'].append(s_st); outs_s['ret'].append(r_st)
        else:
            ow = _odd_layer_params(p, i)
            xp, p_st, c_st = _odd_prompt(xp, g_mix, *ow)
            outs_p['pool'].append(p_st); outs_p['conv'].append(c_st)
            xs, p_st, c_st = _odd_sample(xs, g_mix, *ow, state_pool[i], state_conv[i])
            outs_s['pool'].append(p_st); outs_s['conv'].append(c_st)
        wq = xattn_q[l].astype(BF16)
        wo = xattn_o[l].astype(BF16)
        g_x = norm_cross[l][None]
        xp = _xattn_prompt(xp, g_x, wq, mem_k[l], mem_v[l], wo)
        xs = _xattn_sample(xs, g_x, wq, cache_k[l], cache_v[l], wo)
        g_f = norm_ffn[l][None]
        if l % 2 == 0:
            fw = (ffn_w1[i].astype(BF16), ffn_w3[i].astype(BF16), ffn_w2[i].astype(BF16))
            xp = _ffn(xp.reshape(B * L, D), g_f, *fw, tm=512).reshape(B, L, D)
            xs = _ffn(xs, g_f, *fw, tm=S)
        else:
            mw = _moe_layer_params(p, i)
            xp = _moe(xp.reshape(B * L, D), g_f, *mw, tm=512).reshape(B, L, D)
            xs = _moe(xs, g_f, *mw, tm=S)
    g_fin = norm_final[None]
    y_prompt = _final_norm(xp.reshape(B * L, D), g_fin, tm=1024).reshape(B, L, D)
    y_sample = _final_norm(xs, g_fin, tm=S).reshape(S, 1, D)
    st = lambda d, k: jnp.stack(d[k])
    return (y_prompt, y_sample,
            mem_k.reshape(DEPTH, B, MEM_LEN, X_HEADS, X_HEAD_DIM), mem_v.reshape(DEPTH, B, MEM_LEN, X_HEADS, X_HEAD_DIM),
            st(outs_p, 'ssm'), st(outs_p, 'ssm_conv'), st(outs_p, 'ret'), st(outs_p, 'pool'), st(outs_p, 'conv'),
            st(outs_s, 'ssm'), st(outs_s, 'ssm_conv'), st(outs_s, 'ret'), st(outs_s, 'pool'), st(outs_s, 'conv'))
```

```python
import functools

import jax
import jax.numpy as jnp
import numpy as np
from jax import lax
from jax.experimental import pallas as pl
from jax.experimental.pallas import tpu as pltpu

F32 = jnp.float32
BF16 = jnp.bfloat16
HIGHEST = lax.Precision.HIGHEST

D_MODEL = 1024
DEPTH = 4
PAST_LEN = 16384
EPS = 1e-6
CHUNK = 128
LANES = 128

SSD_HEAD_DIM = 64
SSD_HEADS = 16
SSD_GROUPS = 2
SSD_STATE = 128
SSD_CONV = 4
SSD_CONV_CH = 1536
RET_HEADS = 8
RET_QK_DIM = 64
RET_V_DIM = 128
RET_QK_WIDTH = 512
RET_V_WIDTH = 1024
ROPE_BASE = 10000.0
EVEN_MIX = 2048

POOL_WIDTH = 512
POOL_WINDOWS = (2, 4, 8, 16)
POOL_GROUP_DIM = 128
POOL_STATE = 15
CONV_WIDTH = 512
CONV_K = 31
MEM_LEN = 256
X_HEADS = 4
X_HEAD_DIM = 256
D_FF = 2816
N_EXPERTS = 8
D_FF_EXPERT = 1792

E_Z, E_XBC, E_Q, E_K, E_V, E_G, E_DT, E_END = 0, 1024, 2560, 3072, 3584, 4608, 5632, 5760

VMEM_LIMIT = 56 * 1024 * 1024


def _cparams(*sem):
    return pltpu.CompilerParams(dimension_semantics=sem, vmem_limit_bytes=VMEM_LIMIT)


def _full(shape):
    n = len(shape)
    return pl.BlockSpec(shape, lambda *_: (0,) * n)


def _rms(x, g):
    return x * lax.rsqrt(jnp.mean(x * x, axis=-1, keepdims=True) + EPS) * g


def _rms_only(x):
    return x * lax.rsqrt(jnp.mean(x * x, axis=-1, keepdims=True) + EPS)


def _silu(x):
    return x * jax.nn.sigmoid(x)


def _softplus(x):
    return jnp.maximum(x, 0.0) + jnp.log1p(jnp.exp(-jnp.abs(x)))


def _bdot(a, b):
    return jnp.dot(a.astype(BF16), b.astype(BF16), preferred_element_type=F32)


def _bdot_t(a, b):
    return lax.dot_general(a.astype(BF16), b.astype(BF16), (((1,), (1,)), ((), ())),
                           preferred_element_type=F32)


def _pair_cols(a, p, lo):
    return jnp.where(lo, a[:, 2 * p:2 * p + 1], a[:, 2 * p + 1:2 * p + 2])


def _rotary(x, cos_f, sin_s, lane):
    parts = []
    first_half = (lane % 64) < 32
    for c in range(x.shape[1] // LANES):
        xc = x[:, c * LANES:(c + 1) * LANES]
        parts.append(jnp.where(first_half, pltpu.roll(xc, 96, 1), pltpu.roll(xc, 32, 1)))
    return x * cos_f + jnp.concatenate(parts, axis=1) * sin_s


def _even_prompt_kernel(x_ref, g_ref, win_ref, cw_ref, cb_ref, dtb_ref, alog_ref, dsk_ref, gn_ref,
                        cos_ref, sin_ref, rd_ref, rea_ref, rwe_ref, rgt_ref, wout_ref,
                        xo_ref, nconv_ref, nssm_ref, nret_ref,
                        xbuf, sstate, rstate, *, T):
    t = pl.program_id(1)
    nt = pl.num_programs(1)

    @pl.when(t == 0)
    def _():
        xbuf[0:8, :] = jnp.zeros((8, SSD_CONV_CH), F32)
        sstate[...] = jnp.zeros_like(sstate)
        rstate[...] = jnp.zeros_like(rstate)

    x = x_ref[0]
    hb = _rms(x, g_ref[...]).astype(BF16)

    def proj(a, b):
        return jnp.dot(hb, win_ref[:, a:b], preferred_element_type=F32)

    xbuf[8:8 + T, :] = proj(E_XBC, E_Q)
    conv = cb_ref[...]
    for k in range(SSD_CONV):
        conv = conv + cw_ref[k:k + 1, :] * xbuf[5 + k:5 + k + T, :]
    tail = xbuf[T:T + 8, :]
    xbuf[0:8, :] = tail

    @pl.when(t == nt - 1)
    def _():
        nconv_ref[0] = tail[5:8, :]

    xc = _silu(conv)
    dt = _softplus(proj(E_DT, E_END) + dtb_ref[...])
    la = dt * (-jnp.exp(alog_ref[...]))

    lane = lax.broadcasted_iota(jnp.int32, (CHUNK, LANES), 1)
    row = lax.broadcasted_iota(jnp.int32, (CHUNK, LANES), 0)
    lo = lane < 64
    causal = row >= lane
    ltri = causal.astype(F32)

    q_all = _rotary(proj(E_Q, E_K), cos_ref[...], sin_ref[...], lane[:1])
    k_all = _rotary(proj(E_K, E_V), cos_ref[...], sin_ref[...], lane[:1]) * (RET_QK_DIM ** -0.5)
    v_all = proj(E_V, E_G)

    y_chunks, o_chunks = [], []
    for c in range(T // CHUNK):
        r0 = c * CHUNK
        rows = slice(r0, r0 + CHUNK)
        acum = jnp.dot(ltri, la[rows], precision=HIGHEST, preferred_element_type=F32)
        acum_t = acum.T
        tot = acum[CHUNK - 1:CHUNK, :]
        ea = jnp.exp(acum)
        wend = jnp.exp(tot - acum)
        etot = jnp.exp(tot)
        dt_c = dt[rows]
        y_pairs = []
        for g in range(SSD_GROUPS):
            b_g = xc[rows, 1024 + g * 128:1024 + (g + 1) * 128]
            c_g = xc[rows, 1280 + g * 128:1280 + (g + 1) * 128]
            gram = _bdot_t(c_g, b_g)
            y_inter = _bdot(c_g, sstate[g])
            xw, et = [], []
            for pp in range(4):
                p = g * 4 + pp
                xs_p = xc[rows, p * LANES:(p + 1) * LANES]
                xdt = xs_p * _pair_cols(dt_c, p, lo)
                s_pair = []
                for h in (2 * p, 2 * p + 1):
                    diff = acum[:, h:h + 1] - acum_t[h:h + 1, :]
                    dec = jnp.where(causal, jnp.exp(jnp.minimum(diff, 0.0)), 0.0)
                    s_pair.append((gram * dec).astype(BF16))
                rhs = jnp.concatenate([jnp.where(lo, xdt, 0.0), jnp.where(lo, 0.0, xdt)], axis=0)
                y_p = _bdot(jnp.concatenate(s_pair, axis=1), rhs)
                y_p = y_p + y_inter[:, pp * LANES:(pp + 1) * LANES] * _pair_cols(ea, p, lo)
                y_p = y_p + dsk_ref[:, p * LANES:(p + 1) * LANES] * xs_p
                y_pairs.append(y_p)
                xw.append(xdt * _pair_cols(wend, p, lo))
                et.append(_pair_cols(etot, p, lo[:1]))
            kv = _bdot(b_g.T, jnp.concatenate(xw, axis=1))
            sstate[g] = sstate[g] * jnp.concatenate(et, axis=1) + kv
        y_chunks.append(jnp.concatenate(y_pairs, axis=1))
        o_heads = []
        for p in range(RET_HEADS // 2):
            q_p = q_all[rows, p * LANES:(p + 1) * LANES]
            k_p = k_all[rows, p * LANES:(p + 1) * LANES]
            st_p = rstate[p * LANES:(p + 1) * LANES, :]
            ktw = k_p.T * rwe_ref[p]
            for hh in range(2):
                h = 2 * p + hh
                q_h = jnp.where(lo, q_p, 0.0) if hh == 0 else jnp.where(lo, 0.0, q_p)
                v_h = v_all[rows, h * LANES:(h + 1) * LANES]
                sc = (_bdot_t(q_h, k_p) * rd_ref[h]).astype(BF16)
                o_h = _bdot(sc, v_h) + _bdot(q_h, st_p) * rea_ref[h]
                o_heads.append(_rms_only(o_h))
                r = slice(h * 64, (h + 1) * 64)
                rstate[r, :] = rstate[r, :] * rgt_ref[r, :] + _bdot(ktw[hh * 64:(hh + 1) * 64, :], v_h)
        o_chunks.append(jnp.concatenate(o_heads, axis=1))

    y = jnp.concatenate(y_chunks, axis=0) if len(y_chunks) > 1 else y_chunks[0]
    o = jnp.concatenate(o_chunks, axis=0) if len(o_chunks) > 1 else o_chunks[0]
    y = y * _silu(proj(E_Z, E_XBC))
    y = jnp.concatenate([_rms_only(y[:, :512]), _rms_only(y[:, 512:])], axis=1) * gn_ref[...]
    o = o * _silu(proj(E_G, E_DT))
    mixed = jnp.concatenate([y, o], axis=1).astype(BF16)
    xo_ref[0] = x + jnp.dot(mixed, wout_ref[...], preferred_element_type=F32)

    @pl.when(t == nt - 1)
    def _():
        nssm_ref[0] = sstate[...]
        nret_ref[0] = rstate[...]


def _even_weights(w_in, i):
    w = w_in[i]
    cols = [w[:, 0:1024], w[:, 1024:2560], w[:, 2576:3088], w[:, 3088:3600], w[:, 3600:4624],
            w[:, 4624:5648], w[:, 2560:2576], jnp.zeros((D_MODEL, LANES - SSD_HEADS), w.dtype)]
    return jnp.concatenate(cols, axis=1).astype(BF16)


def _pad_lanes(v):
    return jnp.pad(v.astype(F32), (0, LANES - v.shape[0]))[None, :]


def _even_layer_params(p, i):
    return (_even_weights(p['w_in_even'], i), p['ssd_conv_w'][i], p['ssd_conv_b'][i][None],
            _pad_lanes(p['ssd_dt_bias'][i]), _pad_lanes(p['ssd_a_log'][i]),
            jnp.repeat(p['ssd_d'][i].astype(F32), SSD_HEAD_DIM)[None], p['ssd_norm'][i][None],
            p['w_out_even'][i].astype(BF16))


def _rope_tables(pos):
    half = RET_QK_DIM // 2
    inv = ROPE_BASE ** (-jnp.arange(half, dtype=F32) / half)
    ang = pos[:, None] * inv[None, :]
    cos, sin = jnp.cos(ang), jnp.sin(ang)
    cos_f = jnp.tile(jnp.concatenate([cos, cos], axis=1), (1, RET_HEADS))
    sin_s = jnp.tile(jnp.concatenate([-sin, sin], axis=1), (1, RET_HEADS))
    return cos_f, sin_s


def _ret_consts():
    lg = jnp.log1p(-jnp.exp2(-5.0 - jnp.arange(RET_HEADS, dtype=F32)))
    acum = jnp.cumsum(jnp.broadcast_to(lg[:, None], (RET_HEADS, CHUNK)), axis=1)
    idx = jnp.arange(CHUNK)
    diff = acum[:, :, None] - acum[:, None, :]
    rd = jnp.where((idx[:, None] >= idx[None, :])[None], jnp.exp(jnp.minimum(diff, 0.0)), 0.0)
    rea = jnp.broadcast_to(jnp.exp(acum)[:, :, None], (RET_HEADS, CHUNK, RET_V_DIM))
    total = acum[:, -1]
    we = jnp.exp(total[:, None] - acum)
    rwe = jnp.broadcast_to(we[:, None, :], (RET_HEADS, RET_QK_DIM, CHUNK)).reshape(RET_HEADS // 2, 128, CHUNK)
    rgt = jnp.broadcast_to(jnp.exp(total)[:, None, None], (RET_HEADS, RET_QK_DIM, RET_V_DIM)).reshape(512, RET_V_DIM)
    return rd, rea, rwe, rgt, lg


def _even_prompt(x, g, win, cw, cb, dtb, alog, dsk, gn, wout, cos_f, sin_s, rconsts, T=256):
    B, L, D = x.shape
    rd, rea, rwe, rgt, _ = rconsts
    grid = (B, L // T)
    in_specs = [
        pl.BlockSpec((1, T, D), lambda b, t: (b, t, 0)),
        _full((1, D)), _full(win.shape), _full(cw.shape), _full(cb.shape), _full(dtb.shape),
        _full(alog.shape), _full(dsk.shape), _full(gn.shape),
        pl.BlockSpec((T, RET_QK_WIDTH), lambda b, t: (t, 0)),
        pl.BlockSpec((T, RET_QK_WIDTH), lambda b, t: (t, 0)),
        _full(rd.shape), _full(rea.shape), _full(rwe.shape), _full(rgt.shape), _full(wout.shape),
    ]
    out_shape = [
        jax.ShapeDtypeStruct((B, L, D), F32),
        jax.ShapeDtypeStruct((B, SSD_CONV - 1, SSD_CONV_CH), F32),
        jax.ShapeDtypeStruct((B, SSD_GROUPS, SSD_STATE, 512), F32),
        jax.ShapeDtypeStruct((B, 512, RET_V_DIM), F32),
    ]
    out_specs = [
        pl.BlockSpec((1, T, D), lambda b, t: (b, t, 0)),
        pl.BlockSpec((1, SSD_CONV - 1, SSD_CONV_CH), lambda b, t: (b, 0, 0)),
        pl.BlockSpec((1, SSD_GROUPS, SSD_STATE, 512), lambda b, t: (b, 0, 0, 0)),
        pl.BlockSpec((1, 512, RET_V_DIM), lambda b, t: (b, 0, 0)),
    ]
    scratch = [pltpu.VMEM((T + 8, SSD_CONV_CH), F32), pltpu.VMEM((SSD_GROUPS, SSD_STATE, 512), F32),
               pltpu.VMEM((512, RET_V_DIM), F32)]
    xo, nconv, nssm, nret = pl.pallas_call(
        functools.partial(_even_prompt_kernel, T=T), grid=grid, in_specs=in_specs, out_specs=out_specs,
        out_shape=out_shape, scratch_shapes=scratch, compiler_params=_cparams("parallel", "arbitrary"),
        name="even_prompt",
    )(x, g, win, cw, cb, dtb, alog, dsk, gn, cos_f, sin_s, rd, rea, rwe, rgt, wout)
    nssm = nssm.reshape(B, SSD_GROUPS, SSD_STATE, 8, SSD_HEAD_DIM).transpose(0, 1, 3, 2, 4)
    nssm = nssm.reshape(B, SSD_HEADS, SSD_STATE, SSD_HEAD_DIM)
    nret = nret.reshape(B, RET_HEADS, RET_QK_DIM, RET_V_DIM)
    return xo, nconv, nssm, nret


SSM_ROW = SSD_STATE * SSD_HEAD_DIM
RET_ROW = RET_QK_DIM * RET_V_DIM
SB = 8


def _even_sample_kernel(x_ref, g_ref, win_ref, cw_ref, cb_ref, dtb_ref, alog_ref, dsk_ref, gn_ref,
                        cos_ref, sin_ref, gam_ref, wout_ref, cst_ref, sst_ref, rst_ref,
                        xo_ref, nconv_ref, nsst_ref, nrst_ref,
                        hb_s, xc_s, dt_s, e_s, q_s, k_s, v_s, y_s, o_s):
    i = pl.program_id(0)
    n = pl.num_programs(0)
    lane = lax.broadcasted_iota(jnp.int32, (1, LANES), 1)
    lo = lane < 64

    @pl.when(i == 0)
    def _():
        hb = _rms(x_ref[...], g_ref[...]).astype(BF16)
        hb_s[...] = hb

        def proj(a, b):
            return jnp.dot(hb, win_ref[:, a:b], preferred_element_type=F32)

        xbc = proj(E_XBC, E_Q)
        conv = cb_ref[...] + cw_ref[3:4, :] * xbc
        for k in range(SSD_CONV - 1):
            conv = conv + cw_ref[k:k + 1, :] * cst_ref[0, k]
        nconv_ref[0] = cst_ref[0, 1]
        nconv_ref[1] = cst_ref[0, 2]
        nconv_ref[2] = xbc
        xc_s[...] = _silu(conv)
        dt = _softplus(proj(E_DT, E_END) + dtb_ref[...])
        dt_s[...] = dt
        e_s[...] = jnp.exp(dt * (-jnp.exp(alog_ref[...])))
        q_s[...] = _rotary(proj(E_Q, E_K), cos_ref[...], sin_ref[...], lane)
        k_s[...] = _rotary(proj(E_K, E_V), cos_ref[...], sin_ref[...], lane) * (RET_QK_DIM ** -0.5)
        v_s[...] = proj(E_V, E_G)

    r = pl.ds(pl.multiple_of(i * SB, SB), SB)
    xc = xc_s[r, :]
    dt = dt_s[r, :]
    er = e_s[r, :]
    for g in range(SSD_GROUPS):
        b_g = xc[:, 1024 + g * 128:1024 + (g + 1) * 128]
        c_g = xc[:, 1280 + g * 128:1280 + (g + 1) * 128]
        cb_dot = jnp.sum(c_g * b_g, axis=-1, keepdims=True)
        xps, xreps, ehs = [], [], []
        for hh in range(8):
            h = g * 8 + hh
            p = h // 2
            if hh % 2 == 0:
                xp = xc[:, p * LANES:(p + 1) * LANES] * _pair_cols(dt, p, lo)
                xps.append(xp)
                xreps.append(jnp.where(lo, xp, pltpu.roll(xp, 64, 1)))
            else:
                xreps.append(jnp.where(lo, pltpu.roll(xps[-1], 64, 1), xps[-1]))
            ehs.append(er[:, h:h + 1])
        accs = [jnp.zeros((SB, LANES), F32) for _ in range(8)]
        for r2 in range(SSD_STATE // 2):
            b_p = jnp.where(lo, b_g[:, 2 * r2:2 * r2 + 1], b_g[:, 2 * r2 + 1:2 * r2 + 2])
            c_p = jnp.where(lo, c_g[:, 2 * r2:2 * r2 + 1], c_g[:, 2 * r2 + 1:2 * r2 + 2])
            for hh in range(8):
                col = (g * 8 + hh) * SSM_ROW + r2 * LANES
                blk = sst_ref[:, col:col + LANES]
                accs[hh] = accs[hh] + c_p * blk
                nsst_ref[:, col:col + LANES] = ehs[hh] * blk + b_p * xreps[hh]
        for pp in range(4):
            p = g * 4 + pp
            ya = accs[2 * pp] + pltpu.roll(accs[2 * pp], 64, 1)
            yb = accs[2 * pp + 1] + pltpu.roll(accs[2 * pp + 1], 64, 1)
            y_p = cb_dot * xps[pp] + _pair_cols(er, p, lo) * jnp.where(lo, ya, yb)
            y_p = y_p + dsk_ref[:, p * LANES:(p + 1) * LANES] * xc[:, p * LANES:(p + 1) * LANES]
            y_s[r, p * LANES:(p + 1) * LANES] = y_p
    q = q_s[r, :]
    k = k_s[r, :]
    v = v_s[r, :]
    for h in range(RET_HEADS):
        p = h // 2
        prod = q[:, p * LANES:(p + 1) * LANES] * k[:, p * LANES:(p + 1) * LANES]
        keep = lo if h % 2 == 0 else jnp.logical_not(lo)
        qk = jnp.sum(jnp.where(keep, prod, 0.0), axis=-1, keepdims=True)
        v_h = v[:, h * LANES:(h + 1) * LANES]
        gam = gam_ref[:, h:h + 1]
        acc = jnp.zeros((SB, LANES), F32)
        for kk in range(RET_QK_DIM):
            ln = h * RET_QK_DIM + kk
            col = h * RET_ROW + kk * LANES
            blk = rst_ref[:, col:col + LANES]
            acc = acc + q[:, ln:ln + 1] * blk
            nrst_ref[:, col:col + LANES] = gam * blk + k[:, ln:ln + 1] * v_h
        o_s[r, h * LANES:(h + 1) * LANES] = _rms_only(qk * v_h + gam * acc)

    @pl.when(i == n - 1)
    def _():
        hb = hb_s[...]
        y = y_s[...] * _silu(jnp.dot(hb, win_ref[:, E_Z:E_XBC], preferred_element_type=F32))
        y = jnp.concatenate([_rms_only(y[:, :512]), _rms_only(y[:, 512:])], axis=1) * gn_ref[...]
        o = o_s[...] * _silu(jnp.dot(hb, win_ref[:, E_G:E_DT], preferred_element_type=F32))
        mixed = jnp.concatenate([y, o], axis=1).astype(BF16)
        xo_ref[...] = x_ref[...] + jnp.dot(mixed, wout_ref[...], preferred_element_type=F32)


def _even_sample(x, g, win, cw, cb, dtb, alog, dsk, gn, wout, cos_f, sin_s, gam, conv_all, layer, ssm_st, ret_st):
    S, D = x.shape
    sst = ssm_st.reshape(S, SSD_HEADS * SSM_ROW)
    rst = ret_st.reshape(S, RET_HEADS * RET_ROW)
    cshape = (SSD_CONV - 1, S, SSD_CONV_CH)
    step = lambda w: pl.BlockSpec((SB, w), lambda i: (i, 0))
    in_specs = [_full(x.shape), _full(g.shape), _full(win.shape), _full(cw.shape), _full(cb.shape),
                _full(dtb.shape), _full(alog.shape), _full(dsk.shape), _full(gn.shape),
                _full(cos_f.shape), _full(sin_s.shape), _full(gam.shape), _full(wout.shape),
                pl.BlockSpec((1,) + cshape, lambda i: (layer, 0, 0, 0)), step(sst.shape[1]), step(rst.shape[1])]
    out_shape = [jax.ShapeDtypeStruct(sh, F32) for sh in (x.shape, cshape, sst.shape, rst.shape)]
    out_specs = [_full(x.shape), _full(cshape), step(sst.shape[1]), step(rst.shape[1])]
    scratch = [pltpu.VMEM((S, D), BF16), pltpu.VMEM((S, SSD_CONV_CH), F32), pltpu.VMEM((S, LANES), F32),
               pltpu.VMEM((S, LANES), F32), pltpu.VMEM((S, RET_QK_WIDTH), F32), pltpu.VMEM((S, RET_QK_WIDTH), F32),
               pltpu.VMEM((S, RET_V_WIDTH), F32), pltpu.VMEM((S, D), F32), pltpu.VMEM((S, RET_V_WIDTH), F32)]
    xo, nconv, nsst, nrst = pl.pallas_call(
        _even_sample_kernel, grid=(S // SB,), in_specs=in_specs, out_specs=out_specs, out_shape=out_shape,
        scratch_shapes=scratch, compiler_params=_cparams("arbitrary"), name="even_sample",
    )(x, g, win, cw, cb, dtb, alog, dsk, gn, cos_f, sin_s, gam, wout, conv_all, sst, rst)
    return xo, nconv, nsst.reshape(ssm_st.shape), nrst.reshape(ret_st.shape)


def _layernorm_silu(c, g, b):
    mu = jnp.mean(c, axis=-1, keepdims=True)
    var = jnp.mean(jnp.square(c - mu), axis=-1, keepdims=True)
    return _silu((c - mu) * lax.rsqrt(var + EPS) * g + b)


def _pool_project(parts, pw_ref, ps_ref):
    proj = [_bdot(parts[gi], pw_ref[gi]) for gi in range(len(POOL_WINDOWS))]
    return jnp.concatenate(proj, axis=1) * ps_ref[...]


def _odd_prompt_kernel(x_ref, g_ref, win_ref, pw_ref, ps_ref, cw_ref, cb_ref, cng_ref, cnb_ref, wout_ref,
                       xo_ref, npool_ref, nconv_ref, ubuf, gbuf, *, T):
    t = pl.program_id(1)
    nt = pl.num_programs(1)

    @pl.when(t == 0)
    def _():
        ubuf[0:16, :] = jnp.zeros((16, POOL_WIDTH), F32)
        gbuf[0:32, :] = jnp.zeros((32, CONV_WIDTH), F32)

    x = x_ref[0]
    hb = _rms(x, g_ref[...]).astype(BF16)
    u = jnp.dot(hb, win_ref[:, 0:512], preferred_element_type=F32)
    ga = jnp.dot(hb, win_ref[:, 512:1024], preferred_element_type=F32)
    gb = jnp.dot(hb, win_ref[:, 1024:1536], preferred_element_type=F32)
    ubuf[16:16 + T, :] = u
    pos1 = (lax.broadcasted_iota(jnp.int32, (T, 1), 0) + t * T + 1).astype(F32)
    parts = []
    for gi, w in enumerate(POOL_WINDOWS):
        sl = slice(gi * POOL_GROUP_DIM, (gi + 1) * POOL_GROUP_DIM)
        s = u[:, sl]
        for k in range(1, w):
            s = s + ubuf[16 - k:16 - k + T, sl]
        parts.append(s / jnp.minimum(pos1, float(w)) - u[:, sl])
    pooled = _pool_project(parts, pw_ref, ps_ref)
    utail = ubuf[T:T + 16, :]
    ubuf[0:16, :] = utail
    glu = ga * jax.nn.sigmoid(gb)
    gbuf[32:32 + T, :] = glu
    c = cb_ref[...] + cw_ref[CONV_K - 1:CONV_K, :] * glu
    for k in range(CONV_K - 1):
        c = c + cw_ref[k:k + 1, :] * gbuf[2 + k:2 + k + T, :]
    gtail = gbuf[T:T + 32, :]
    gbuf[0:32, :] = gtail
    c = _layernorm_silu(c, cng_ref[...], cnb_ref[...])
    mixed = jnp.concatenate([pooled, c], axis=1).astype(BF16)
    xo_ref[0] = x + jnp.dot(mixed, wout_ref[...], preferred_element_type=F32)

    @pl.when(t == nt - 1)
    def _():
        npool_ref[0] = utail[1:16, :]
        nconv_ref[0] = gtail[2:32, :]


def _odd_layer_params(p, i):
    return (p['w_in_odd'][i].astype(BF16), p['pool_w'][i].astype(BF16), p['pool_scale'][i][None],
            p['conv_w'][i], p['conv_b'][i][None], p['conv_norm_g'][i][None], p['conv_norm_b'][i][None],
            p['w_out_odd'][i].astype(BF16))


def _odd_prompt(x, g, win, pw, ps, cw, cb, cng, cnb, wout, T=256):
    B, L, D = x.shape
    consts = (g, win, pw, ps, cw, cb, cng, cnb, wout)
    in_specs = [pl.BlockSpec((1, T, D), lambda b, t: (b, t, 0))] + [_full(a.shape) for a in consts]
    out_shape = [jax.ShapeDtypeStruct((B, L, D), F32), jax.ShapeDtypeStruct((B, POOL_STATE, POOL_WIDTH), F32),
                 jax.ShapeDtypeStruct((B, CONV_K - 1, CONV_WIDTH), F32)]
    out_specs = [pl.BlockSpec((1, T, D), lambda b, t: (b, t, 0)),
                 pl.BlockSpec((1, POOL_STATE, POOL_WIDTH), lambda b, t: (b, 0, 0)),
                 pl.BlockSpec((1, CONV_K - 1, CONV_WIDTH), lambda b, t: (b, 0, 0))]
    scratch = [pltpu.VMEM((T + 16, POOL_WIDTH), F32), pltpu.VMEM((T + 32, CONV_WIDTH), F32)]
    return pl.pallas_call(
        functools.partial(_odd_prompt_kernel, T=T), grid=(B, L // T), in_specs=in_specs, out_specs=out_specs,
        out_shape=out_shape, scratch_shapes=scratch, compiler_params=_cparams("parallel", "arbitrary"),
        name="odd_prompt",
    )(x, *consts)


def _odd_sample_kernel(x_ref, g_ref, win_ref, pw_ref, ps_ref, cw_ref, cb_ref, cng_ref, cnb_ref, wout_ref,
                       pst_ref, cst_ref, xo_ref, npool_ref, nconv_ref):
    x = x_ref[...]
    hb = _rms(x, g_ref[...]).astype(BF16)
    u = jnp.dot(hb, win_ref[:, 0:512], preferred_element_type=F32)
    ga = jnp.dot(hb, win_ref[:, 512:1024], preferred_element_type=F32)
    gb = jnp.dot(hb, win_ref[:, 1024:1536], preferred_element_type=F32)
    parts = []
    for gi, w in enumerate(POOL_WINDOWS):
        lo_, hi_ = gi * POOL_GROUP_DIM, (gi + 1) * POOL_GROUP_DIM
        s = u[:, lo_:hi_]
        for k in range(1, w):
            s = s + pst_ref[0, POOL_STATE - k, :, lo_:hi_]
        parts.append(s / float(w) - u[:, lo_:hi_])
    pooled = _pool_project(parts, pw_ref, ps_ref)
    for j in range(POOL_STATE - 1):
        npool_ref[j] = pst_ref[0, j + 1]
    npool_ref[POOL_STATE - 1] = u
    glu = ga * jax.nn.sigmoid(gb)
    c = cb_ref[...] + cw_ref[CONV_K - 1:CONV_K, :] * glu
    for k in range(CONV_K - 1):
        c = c + cw_ref[k:k + 1, :] * cst_ref[0, k]
    for j in range(CONV_K - 2):
        nconv_ref[j] = cst_ref[0, j + 1]
    nconv_ref[CONV_K - 2] = glu
    c = _layernorm_silu(c, cng_ref[...], cnb_ref[...])
    mixed = jnp.concatenate([pooled, c], axis=1).astype(BF16)
    xo_ref[...] = x + jnp.dot(mixed, wout_ref[...], preferred_element_type=F32)


def _odd_sample(x, g, win, pw, ps, cw, cb, cng, cnb, wout, pool_all, conv_all, layer):
    S, D = x.shape
    pshape = (POOL_STATE, S, POOL_WIDTH)
    cshape = (CONV_K - 1, S, CONV_WIDTH)
    consts = (x, g, win, pw, ps, cw, cb, cng, cnb, wout)
    lspec = lambda sh: pl.BlockSpec((1,) + sh, lambda i: (layer, 0, 0, 0))
    outs = (x.shape, pshape, cshape)
    return pl.pallas_call(
        _odd_sample_kernel, grid=(1,), in_specs=[_full(a.shape) for a in consts] + [lspec(pshape), lspec(cshape)],
        out_specs=[_full(sh) for sh in outs], out_shape=[jax.ShapeDtypeStruct(sh, F32) for sh in outs],
        compiler_params=_cparams("arbitrary"), name="odd_sample",
    )(*consts, pool_all, conv_all)


def _mem_kv_kernel(m_ref, g_ref, w_ref, k_ref, v_ref):
    hb = _rms(m_ref[...], g_ref[0]).astype(BF16)
    kv = jnp.dot(hb, w_ref[0], preferred_element_type=F32)
    k_ref[0] = kv[:, :D_MODEL]
    v_ref[0] = kv[:, D_MODEL:]


def _mem_kv(mem, norm_mem, wkv, tm=512):
    M, D = mem.shape
    out = jax.ShapeDtypeStruct((DEPTH, M, D), F32)
    return pl.pallas_call(
        _mem_kv_kernel, grid=(DEPTH, M // tm),
        in_specs=[pl.BlockSpec((tm, D), lambda l, m: (m, 0)), pl.BlockSpec((1, 1, D), lambda l, m: (l, 0, 0)),
                  pl.BlockSpec((1, D, 2 * D), lambda l, m: (l, 0, 0))],
        out_specs=[pl.BlockSpec((1, tm, D), lambda l, m: (l, m, 0))] * 2, out_shape=[out, out],
        compiler_params=_cparams("parallel", "parallel"), name="mem_kv",
    )(mem, norm_mem[:, None, :], wkv)


def _xattn_prompt_kernel(x_ref, g_ref, wq_ref, k_ref, v_ref, wo_ref, xo_ref):
    x = x_ref[0]
    hb = _rms(x, g_ref[...]).astype(BF16)
    q = jnp.dot(hb, wq_ref[...], preferred_element_type=F32) * (X_HEAD_DIM ** -0.5)
    outs = []
    for h in range(X_HEADS):
        sl = slice(h * X_HEAD_DIM, (h + 1) * X_HEAD_DIM)
        s = _bdot_t(q[:, sl], k_ref[0, :, sl])
        p = jnp.exp(s - jnp.max(s, axis=-1, keepdims=True))
        outs.append(_bdot(p, v_ref[0, :, sl]) / jnp.sum(p, axis=-1, keepdims=True))
    o = jnp.concatenate(outs, axis=1).astype(BF16)
    xo_ref[0] = x + jnp.dot(o, wo_ref[...], preferred_element_type=F32)


def _xattn_prompt(x, g, wq, mk, mv, wo, T=512):
    B, L, D = x.shape
    xspec = pl.BlockSpec((1, T, D), lambda b, t: (b, t, 0))
    mspec = pl.BlockSpec((1, MEM_LEN, D), lambda b, t: (b, 0, 0))
    return pl.pallas_call(
        _xattn_prompt_kernel, grid=(B, L // T),
        in_specs=[xspec, _full(g.shape), _full(wq.shape), mspec, mspec, _full(wo.shape)],
        out_specs=xspec, out_shape=jax.ShapeDtypeStruct(x.shape, F32),
        compiler_params=_cparams("parallel", "parallel"), name="xattn_prompt",
    )(x, g, wq, mk, mv, wo)


XB = 4


def _xattn_sample_kernel(x_ref, g_ref, wq_ref, k_ref, v_ref, wo_ref, xo_ref, q_s, o_s):
    i = pl.program_id(0)
    n = pl.num_programs(0)

    @pl.when(i == 0)
    def _():
        hb = _rms(x_ref[...], g_ref[...]).astype(BF16)
        q_s[...] = jnp.dot(hb, wq_ref[...], preferred_element_type=F32) * (X_HEAD_DIM ** -0.5)

    for j in range(XB):
        r = pl.ds(i * XB + j, 1)
        qrow = q_s[r, :]
        q8 = jnp.concatenate([qrow[:, c * LANES:(c + 1) * LANES] for c in range(8)], axis=0)
        part = jnp.sum(k_ref[0, j] * q8[None], axis=-1, keepdims=True)
        s = part + pltpu.roll(part, 4, 1)
        p = jnp.exp(s - jnp.max(s, axis=0, keepdims=True))
        o8 = jnp.sum(p * v_ref[0, j], axis=0) / jnp.sum(p, axis=0)
        o_s[r, :] = jnp.concatenate([o8[c:c + 1, :] for c in range(8)], axis=1)

    @pl.when(i == n - 1)
    def _():
        xo_ref[...] = x_ref[...] + jnp.dot(o_s[...].astype(BF16), wo_ref[...], preferred_element_type=F32)


def _cache_tiles(c):
    d, s, m = c.shape[:3]
    return c.reshape(d, s, m, X_HEADS, 2, LANES).transpose(0, 1, 2, 4, 3, 5).reshape(d, s, m, 8, LANES)


def _xattn_perm_weights(wq, wo):
    d = wq.shape[0]
    wq_p = wq.reshape(d, X_HEADS, 2, LANES).transpose(0, 2, 1, 3).reshape(d, d)
    wo_p = wo.reshape(X_HEADS, 2, LANES, d).transpose(1, 0, 2, 3).reshape(d, d)
    return wq_p, wo_p


def _xattn_sample(x, g, wq_p, ck, cv, wo_p, l):
    S, D = x.shape
    cspec = pl.BlockSpec((1, XB, MEM_LEN, 8, LANES), lambda i: (l, i, 0, 0, 0))
    return pl.pallas_call(
        _xattn_sample_kernel, grid=(S // XB,),
        in_specs=[_full(x.shape), _full(g.shape), _full(wq_p.shape), cspec, cspec, _full(wo_p.shape)],
        out_specs=_full(x.shape), out_shape=jax.ShapeDtypeStruct(x.shape, F32),
        scratch_shapes=[pltpu.VMEM((S, D), F32), pltpu.VMEM((S, D), F32)],
        compiler_params=_cparams("arbitrary"), name="xattn_sample",
    )(x, g, wq_p, ck, cv, wo_p)


def _ffn_kernel(x_ref, g_ref, w1_ref, w3_ref, w2_ref, xo_ref):
    x = x_ref[...]
    hb = _rms(x, g_ref[...]).astype(BF16)
    a = jnp.dot(hb, w1_ref[...], preferred_element_type=F32)
    b = jnp.dot(hb, w3_ref[...], preferred_element_type=F32)
    xo_ref[...] = x + jnp.dot((_silu(a) * b).astype(BF16), w2_ref[...], preferred_element_type=F32)


def _ffn(x, g, w1, w3, w2, tm):
    M, D = x.shape
    xspec = pl.BlockSpec((tm, D), lambda m: (m, 0))
    return pl.pallas_call(
        _ffn_kernel, grid=(M // tm,),
        in_specs=[xspec, _full(g.shape), _full(w1.shape), _full(w3.shape), _full(w2.shape)],
        out_specs=xspec, out_shape=jax.ShapeDtypeStruct(x.shape, F32),
        compiler_params=_cparams("parallel"), name="ffn",
    )(x, g, w1, w3, w2)


def _moe_kernel(x_ref, g_ref, rw_ref, rb_ref, w1_ref, w3_ref, w2_ref, xo_ref, hb_s, comb_s, acc_s):
    e = pl.program_id(1)
    lane = lax.broadcasted_iota(jnp.int32, (1, LANES), 1)

    @pl.when(e == 0)
    def _():
        h = _rms(x_ref[...], g_ref[...])
        hb_s[...] = h.astype(BF16)
        logits = jnp.dot(h, rw_ref[...], precision=HIGHEST, preferred_element_type=F32) + rb_ref[...]
        m1 = jnp.max(logits, axis=-1, keepdims=True)
        i1 = jnp.min(jnp.where(logits == m1, lane, LANES), axis=-1, keepdims=True)
        rest = jnp.where(lane == i1, -jnp.inf, logits)
        m2 = jnp.max(rest, axis=-1, keepdims=True)
        i2 = jnp.min(jnp.where(rest == m2, lane, LANES), axis=-1, keepdims=True)
        e2 = jnp.exp(m2 - m1)
        comb_s[...] = jnp.where(lane == i1, 1.0 / (1.0 + e2), 0.0) + jnp.where(lane == i2, e2 / (1.0 + e2), 0.0)
        acc_s[...] = jnp.zeros_like(acc_s)

    hb = hb_s[...]
    a = jnp.dot(hb, w1_ref[0], preferred_element_type=F32)
    b = jnp.dot(hb, w3_ref[0], preferred_element_type=F32)
    y = jnp.dot((_silu(a) * b).astype(BF16), w2_ref[0], preferred_element_type=F32)
    gate = jnp.sum(jnp.where(lane == e, comb_s[...], 0.0), axis=-1, keepdims=True)
    acc_s[...] += gate * y

    @pl.when(e == pl.num_programs(1) - 1)
    def _():
        xo_ref[...] = x_ref[...] + acc_s[...]


def _moe_layer_params(p, i):
    rw = jnp.pad(p['router_w'][i].astype(F32), ((0, 0), (0, LANES - N_EXPERTS)))
    rb = jnp.concatenate([p['router_b'][i].astype(F32), jnp.full((LANES - N_EXPERTS,), -1e30, F32)])[None]
    return rw, rb, p['moe_w1'][i].astype(BF16), p['moe_w3'][i].astype(BF16), p['moe_w2'][i].astype(BF16)


def _moe(x, g, rw, rb, w1, w3, w2, tm):
    M, D = x.shape
    xspec = pl.BlockSpec((tm, D), lambda m, e: (m, 0))
    return pl.pallas_call(
        _moe_kernel, grid=(M // tm, N_EXPERTS),
        in_specs=[xspec, _full(g.shape), _full(rw.shape), _full(rb.shape),
                  pl.BlockSpec((1, D, D_FF_EXPERT), lambda m, e: (e, 0, 0)),
                  pl.BlockSpec((1, D, D_FF_EXPERT), lambda m, e: (e, 0, 0)),
                  pl.BlockSpec((1, D_FF_EXPERT, D), lambda m, e: (e, 0, 0))],
        out_specs=xspec, out_shape=jax.ShapeDtypeStruct(x.shape, F32),
        scratch_shapes=[pltpu.VMEM((tm, D), BF16), pltpu.VMEM((tm, LANES), F32), pltpu.VMEM((tm, D), F32)],
        compiler_params=_cparams("parallel", "arbitrary"), name="moe",
    )(x, g, rw, rb, w1, w3, w2)


MOE_TM = 512


def _moe_route_kernel(x_ref, g_ref, rw_ref, rb_ref, meta_ref, cnt_ref, base_s, *, tm):
    @pl.when(pl.program_id(0) == 0)
    def _():
        base_s[...] = jnp.zeros_like(base_s)

    lane = lax.broadcasted_iota(jnp.int32, (1, LANES), 1)
    h = _rms(x_ref[...], g_ref[...])
    logits = jnp.dot(h, rw_ref[...], precision=HIGHEST, preferred_element_type=F32) + rb_ref[...]
    m1 = jnp.max(logits, axis=-1, keepdims=True)
    i1 = jnp.min(jnp.where(logits == m1, lane, LANES), axis=-1, keepdims=True)
    rest = jnp.where(lane == i1, -jnp.inf, logits)
    m2 = jnp.max(rest, axis=-1, keepdims=True)
    i2 = jnp.min(jnp.where(rest == m2, lane, LANES), axis=-1, keepdims=True)
    e2 = jnp.exp(m2 - m1)
    oh1 = lane == i1
    oh2 = lane == i2
    oh = jnp.logical_or(oh1, oh2).astype(F32)
    strict = (lax.broadcasted_iota(jnp.int32, (tm, tm), 0) > lax.broadcasted_iota(jnp.int32, (tm, tm), 1))
    pos = jnp.dot(strict.astype(BF16), oh.astype(BF16), preferred_element_type=F32) + base_s[...]
    r1 = jnp.sum(jnp.where(oh1, pos, 0.0), axis=-1, keepdims=True)
    r2 = jnp.sum(jnp.where(oh2, pos, 0.0), axis=-1, keepdims=True)
    base_s[...] += jnp.sum(oh, axis=0, keepdims=True)
    cols = (i1.astype(F32), i2.astype(F32), r1, r2, 1.0 / (1.0 + e2), e2 / (1.0 + e2))
    meta = jnp.zeros((tm, LANES), F32)
    for c, val in enumerate(cols):
        meta = jnp.where(lane == c, val, meta)
    meta_ref[...] = meta
    cnt_ref[...] = base_s[...]


def _moe_dispatch_kernel(dest_ref, x_ref, g_ref, xg_ref, hbuf, sem, *, tm, M):
    base = pl.program_id(0) * tm
    hbuf[...] = _rms(x_ref[...], g_ref[...])

    def body(r, c):
        for k in range(2):
            d = dest_ref[k * M + base + r]
            pltpu.make_async_copy(hbuf.at[pl.ds(r, 1)], xg_ref.at[pl.ds(d, 1)], sem).start()
        return c

    lax.fori_loop(0, tm, body, 0, unroll=8)
    for k in range(2):
        pltpu.make_async_copy(hbuf, xg_ref.at[pl.ds(0, tm)], sem).wait()


def _moe_group_kernel(te_ref, tb_ref, nv_ref, tot_ref, xg_ref, w1_ref, w3_ref, w2_ref, y_ref):
    s = pl.program_id(0)

    @pl.when(s < tot_ref[0])
    def _():
        rows = lax.broadcasted_iota(jnp.int32, (MOE_TM, 1), 0)
        xt = jnp.where(rows < nv_ref[s], xg_ref[...], 0.0).astype(BF16)
        a = jnp.dot(xt, w1_ref[0], preferred_element_type=F32)
        b = jnp.dot(xt, w3_ref[0], preferred_element_type=F32)
        y_ref[...] = jnp.dot((_silu(a) * b).astype(BF16), w2_ref[0], preferred_element_type=F32)


def _moe_combine_kernel(dest_ref, x_ref, meta_ref, gfin_ref, y_ref, o_ref, ybuf, sem, *, tm, M, final):
    base = pl.program_id(0) * tm

    def body(r, c):
        for k in range(2):
            d = dest_ref[k * M + base + r]
            pltpu.make_async_copy(y_ref.at[pl.ds(d, 1)], ybuf.at[k, pl.ds(r, 1)], sem).start()
        return c

    lax.fori_loop(0, tm, body, 0, unroll=8)
    for k in range(2):
        pltpu.make_async_copy(y_ref.at[pl.ds(0, tm)], ybuf.at[k], sem).wait()
    out = x_ref[...] + (meta_ref[:, 4:5] * ybuf[0] + meta_ref[:, 5:6] * ybuf[1])
    o_ref[...] = _rms(out, gfin_ref[...]) if final else out


def _moe_routed(x, g, rw, rb, w1, w3, w2, gfin=None, tm=512, tmd=256):
    M, D = x.shape
    TM = MOE_TM
    NT = 2 * M // TM + N_EXPERTS
    R = NT * TM
    xspec = pl.BlockSpec((tm, D), lambda m: (m, 0))
    meta, cnt = pl.pallas_call(
        functools.partial(_moe_route_kernel, tm=tm), grid=(M // tm,),
        in_specs=[xspec, _full(g.shape), _full(rw.shape), _full(rb.shape)],
        out_specs=[pl.BlockSpec((tm, LANES), lambda m: (m, 0)), _full((1, LANES))],
        out_shape=[jax.ShapeDtypeStruct((M, LANES), F32), jax.ShapeDtypeStruct((1, LANES), F32)],
        scratch_shapes=[pltpu.VMEM((1, LANES), F32)], compiler_params=_cparams("arbitrary"), name="moe_route",
    )(x, g, rw, rb)
    cnt = cnt[0, :N_EXPERTS].astype(jnp.int32)
    ntile = (cnt + TM - 1) // TM
    tile_end = jnp.cumsum(ntile)
    tile_start = tile_end - ntile
    total = tile_end[-1]
    s = jnp.arange(NT, dtype=jnp.int32)
    tb = jnp.minimum(s, total - 1)
    te = jnp.minimum(jnp.searchsorted(tile_end, tb, side='right'), N_EXPERTS - 1).astype(jnp.int32)
    nv = jnp.clip(cnt[te] - (tb - tile_start[te]) * TM, 0, TM).astype(jnp.int32)
    row_off = (tile_start * TM).astype(jnp.int32)
    dest = jnp.concatenate([row_off[meta[:, 0].astype(jnp.int32)] + meta[:, 2].astype(jnp.int32),
                            row_off[meta[:, 1].astype(jnp.int32)] + meta[:, 3].astype(jnp.int32)])
    xdspec = pl.BlockSpec((tmd, D), lambda m, *_: (m, 0))
    xg = pl.pallas_call(
        functools.partial(_moe_dispatch_kernel, tm=tmd, M=M),
        grid_spec=pltpu.PrefetchScalarGridSpec(
            num_scalar_prefetch=1, grid=(M // tmd,),
            in_specs=[xdspec, pl.BlockSpec(g.shape, lambda m, *_: (0, 0))],
            out_specs=pl.BlockSpec(memory_space=pl.ANY),
            scratch_shapes=[pltpu.VMEM((tmd, D), F32), pltpu.SemaphoreType.DMA(())]),
        out_shape=jax.ShapeDtypeStruct((R, D), F32), compiler_params=_cparams("arbitrary"), name="moe_dispatch",
    )(dest, x, g)
    wspec = lambda shape: pl.BlockSpec((1,) + shape, lambda s, te, tb, nv, tot: (te[s], 0, 0))
    tspec = pl.BlockSpec((TM, D), lambda s, te, tb, nv, tot: (tb[s], 0))
    y = pl.pallas_call(
        _moe_group_kernel,
        grid_spec=pltpu.PrefetchScalarGridSpec(
            num_scalar_prefetch=4, grid=(NT,),
            in_specs=[tspec, wspec((D, D_FF_EXPERT)), wspec((D, D_FF_EXPERT)), wspec((D_FF_EXPERT, D))],
            out_specs=tspec),
        out_shape=jax.ShapeDtypeStruct((R, D), F32), compiler_params=_cparams("arbitrary"), name="moe_group",
    )(te, tb, nv, total.reshape(1).astype(jnp.int32), xg, w1, w3, w2)
    final = gfin is not None
    gfin = g if gfin is None else gfin
    return pl.pallas_call(
        functools.partial(_moe_combine_kernel, tm=tmd, M=M, final=final),
        grid_spec=pltpu.PrefetchScalarGridSpec(
            num_scalar_prefetch=1, grid=(M // tmd,),
            in_specs=[xdspec, pl.BlockSpec((tmd, LANES), lambda m, *_: (m, 0)),
                      pl.BlockSpec(g.shape, lambda m, *_: (0, 0)), pl.BlockSpec(memory_space=pl.ANY)],
            out_specs=xdspec,
            scratch_shapes=[pltpu.VMEM((2, tmd, D), F32), pltpu.SemaphoreType.DMA(())]),
        out_shape=jax.ShapeDtypeStruct((M, D), F32), compiler_params=_cparams("arbitrary"), name="moe_combine",
    )(dest, x, meta, gfin, y)


def _final_norm_kernel(x_ref, g_ref, o_ref):
    o_ref[...] = _rms(x_ref[...], g_ref[...])


def _final_norm(x, g, tm):
    M, D = x.shape
    spec = pl.BlockSpec((tm, D), lambda m: (m, 0))
    return pl.pallas_call(
        _final_norm_kernel, grid=(M // tm,), in_specs=[spec, _full(g.shape)], out_specs=spec,
        out_shape=jax.ShapeDtypeStruct(x.shape, F32), compiler_params=_cparams("parallel"), name="final_norm",
    )(x, g)


def kernel(x_prompt, x_sample, state_ssm, state_ssm_conv, state_ret, state_pool, state_conv, cache_mem_k, cache_mem_v, mem_prompt, norm_mix, norm_cross, norm_ffn, norm_mem, norm_final, w_in_even, ssd_conv_w, ssd_conv_b, ssd_dt_bias, ssd_a_log, ssd_d, ssd_norm, w_out_even, ffn_w1, ffn_w3, ffn_w2, w_in_odd, pool_w, pool_scale, conv_w, conv_b, conv_norm_g, conv_norm_b, w_out_odd, router_w, router_b, moe_w1, moe_w3, moe_w2, xattn_q, xattn_k, xattn_v, xattn_o):
    p = dict(norm_mix=norm_mix, norm_cross=norm_cross, norm_ffn=norm_ffn, norm_mem=norm_mem,
             w_in_even=w_in_even, ssd_conv_w=ssd_conv_w, ssd_conv_b=ssd_conv_b, ssd_dt_bias=ssd_dt_bias,
             ssd_a_log=ssd_a_log, ssd_d=ssd_d, ssd_norm=ssd_norm, w_out_even=w_out_even,
             ffn_w1=ffn_w1, ffn_w3=ffn_w3, ffn_w2=ffn_w2,
             w_in_odd=w_in_odd, pool_w=pool_w, pool_scale=pool_scale, conv_w=conv_w, conv_b=conv_b,
             conv_norm_g=conv_norm_g, conv_norm_b=conv_norm_b, w_out_odd=w_out_odd,
             router_w=router_w, router_b=router_b, moe_w1=moe_w1, moe_w3=moe_w3, moe_w2=moe_w2,
             xattn_q=xattn_q, xattn_k=xattn_k, xattn_v=xattn_v, xattn_o=xattn_o)
    B, L, D = x_prompt.shape
    S = x_sample.shape[0]
    rconsts = _ret_consts()
    cos_p, sin_p = _rope_tables(jnp.arange(L, dtype=F32))
    cos_s, sin_s = _rope_tables(PAST_LEN + jnp.arange(1, dtype=F32))
    gam = _pad_lanes(jnp.exp(rconsts[4]))

    mem_k, mem_v = _mem_kv(mem_prompt.reshape(B * MEM_LEN, D), norm_mem,
                           jnp.concatenate([xattn_k, xattn_v], axis=2).astype(BF16))
    mem_k = mem_k.reshape(DEPTH, B, MEM_LEN, D)
    mem_v = mem_v.reshape(DEPTH, B, MEM_LEN, D)
    cache_k = _cache_tiles(cache_mem_k)
    cache_v = _cache_tiles(cache_mem_v)
    sconv_rows = state_ssm_conv.transpose(0, 2, 1, 3)
    pool_rows = state_pool.transpose(0, 2, 1, 3)
    conv_rows = state_conv.transpose(0, 2, 1, 3)

    xp = x_prompt
    xs = x_sample.reshape(S, D)
    outs_p = dict(ssm=[], ssm_conv=[], ret=[], pool=[], conv=[])
    outs_s = dict(ssm=[], ssm_conv=[], ret=[], pool=[], conv=[])
    for l in range(DEPTH):
        i = l // 2
        g_mix = norm_mix[l][None]
        if l % 2 == 0:
            ew = _even_layer_params(p, i)
            xp, c_st, s_st, r_st = _even_prompt(xp, g_mix, *ew, cos_p, sin_p, rconsts)
            outs_p['ssm_conv'].append(c_st); outs_p['ssm'].append(s_st); outs_p['ret'].append(r_st)
            xs, c_st, s_st, r_st = _even_sample(xs, g_mix, *ew, cos_s, sin_s, gam,
                                                sconv_rows, i, state_ssm[i], state_ret[i])
            outs_s['ssm_conv'].append(c_st); outs_s['ssm'].append(s_st); outs_s['ret'].append(r_st)
        else:
            ow = _odd_layer_params(p, i)
            xp, p_st, c_st = _odd_prompt(xp, g_mix, *ow)
            outs_p['pool'].append(p_st); outs_p['conv'].append(c_st)
            xs, p_st, c_st = _odd_sample(xs, g_mix, *ow, pool_rows, conv_rows, i)
            outs_s['pool'].append(p_st); outs_s['conv'].append(c_st)
        wq = xattn_q[l].astype(BF16)
        wo = xattn_o[l].astype(BF16)
        g_x = norm_cross[l][None]
        xp = _xattn_prompt(xp, g_x, wq, mem_k[l], mem_v[l], wo)
        wq_p, wo_p = _xattn_perm_weights(wq, wo)
        xs = _xattn_sample(xs, g_x, wq_p, cache_k, cache_v, wo_p, l)
        g_f = norm_ffn[l][None]
        if l % 2 == 0:
            fw = (ffn_w1[i].astype(BF16), ffn_w3[i].astype(BF16), ffn_w2[i].astype(BF16))
            xp = _ffn(xp.reshape(B * L, D), g_f, *fw, tm=512).reshape(B, L, D)
            xs = _ffn(xs, g_f, *fw, tm=S)
        else:
            mw = _moe_layer_params(p, i)
            gfin = norm_final[None] if l == DEPTH - 1 else None
            xp = _moe_routed(xp.reshape(B * L, D), g_f, *mw, gfin=gfin).reshape(B, L, D)
            xs = _moe(xs, g_f, *mw, tm=S)
    g_fin = norm_final[None]
    y_prompt = xp
    y_sample = _final_norm(xs, g_fin, tm=S).reshape(S, 1, D)
    st = lambda d, k: jnp.stack(d[k])
    st_rows = lambda d, k: jnp.stack(d[k]).transpose(0, 2, 1, 3)
    return (y_prompt, y_sample,
            mem_k.reshape(DEPTH, B, MEM_LEN, X_HEADS, X_HEAD_DIM), mem_v.reshape(DEPTH, B, MEM_LEN, X_HEADS, X_HEAD_DIM),
            st(outs_p, 'ssm'), st(outs_p, 'ssm_conv'), st(outs_p, 'ret'), st(outs_p, 'pool'), st(outs_p, 'conv'),
            st(outs_s, 'ssm'), st_rows(outs_s, 'ssm_conv'), st(outs_s, 'ret'), st_rows(outs_s, 'pool'),
            st_rows(outs_s, 'conv'))
```

```python
import functools

import jax
import jax.numpy as jnp
import numpy as np
from jax import lax
from jax.experimental import pallas as pl
from jax.experimental.pallas import tpu as pltpu

F32 = jnp.float32
BF16 = jnp.bfloat16
HIGHEST = lax.Precision.HIGHEST

D_MODEL = 1024
DEPTH = 4
PAST_LEN = 16384
EPS = 1e-6
CHUNK = 128
LANES = 128

SSD_HEAD_DIM = 64
SSD_HEADS = 16
SSD_GROUPS = 2
SSD_STATE = 128
SSD_CONV = 4
SSD_CONV_CH = 1536
RET_HEADS = 8
RET_QK_DIM = 64
RET_V_DIM = 128
RET_QK_WIDTH = 512
RET_V_WIDTH = 1024
ROPE_BASE = 10000.0
EVEN_MIX = 2048

POOL_WIDTH = 512
POOL_WINDOWS = (2, 4, 8, 16)
POOL_GROUP_DIM = 128
POOL_STATE = 15
CONV_WIDTH = 512
CONV_K = 31
MEM_LEN = 256
X_HEADS = 4
X_HEAD_DIM = 256
D_FF = 2816
N_EXPERTS = 8
D_FF_EXPERT = 1792

E_Z, E_XBC, E_Q, E_K, E_V, E_G, E_DT, E_END = 0, 1024, 2560, 3072, 3584, 4608, 5632, 5760

VMEM_LIMIT = 56 * 1024 * 1024


def _cparams(*sem):
    return pltpu.CompilerParams(dimension_semantics=sem, vmem_limit_bytes=VMEM_LIMIT)


def _full(shape):
    n = len(shape)
    return pl.BlockSpec(shape, lambda *_: (0,) * n)


class _Layer:
    def __init__(self, arr, idx):
        self.arr, self.idx = arr, idx

    @property
    def spec(self):
        idx, rest = self.idx, (0,) * (self.arr.ndim - 1)
        return pl.BlockSpec((None,) + self.arr.shape[1:], lambda *_: (idx,) + rest, pipeline_mode=pl.Buffered(1))


def _specs(ops):
    return [o.spec if isinstance(o, _Layer) else _full(o.shape) for o in ops]


def _arrs(ops):
    return [o.arr if isinstance(o, _Layer) else o for o in ops]


def _rms(x, g):
    return x * lax.rsqrt(jnp.mean(x * x, axis=-1, keepdims=True) + EPS) * g


def _rms_only(x):
    return x * lax.rsqrt(jnp.mean(x * x, axis=-1, keepdims=True) + EPS)


def _silu(x):
    return x * jax.nn.sigmoid(x)


def _softplus(x):
    return jnp.maximum(x, 0.0) + jnp.log1p(jnp.exp(-jnp.abs(x)))


def _bdot(a, b):
    return jnp.dot(a.astype(BF16), b.astype(BF16), preferred_element_type=F32)


def _bdot_t(a, b):
    return lax.dot_general(a.astype(BF16), b.astype(BF16), (((1,), (1,)), ((), ())),
                           preferred_element_type=F32)


def _pair_cols(a, p, lo):
    return jnp.where(lo, a[:, 2 * p:2 * p + 1], a[:, 2 * p + 1:2 * p + 2])


def _rotary(x, cos_f, sin_s, lane):
    parts = []
    first_half = (lane % 64) < 32
    for c in range(x.shape[1] // LANES):
        xc = x[:, c * LANES:(c + 1) * LANES]
        parts.append(jnp.where(first_half, pltpu.roll(xc, 96, 1), pltpu.roll(xc, 32, 1)))
    return x * cos_f + jnp.concatenate(parts, axis=1) * sin_s


def _even_prompt_kernel(x_ref, g_ref, win_ref, cw_ref, cb_ref, dtb_ref, alog_ref, dsk_ref, gn_ref,
                        cos_ref, sin_ref, rd_ref, rea_ref, rwe_ref, rgt_ref, wout_ref,
                        xo_ref, nconv_ref, nssm_ref, nret_ref,
                        xbuf, sstate, rstate, *, T):
    t = pl.program_id(1)
    nt = pl.num_programs(1)

    @pl.when(t == 0)
    def _():
        xbuf[0:8, :] = jnp.zeros((8, SSD_CONV_CH), F32)
        sstate[...] = jnp.zeros_like(sstate)
        rstate[...] = jnp.zeros_like(rstate)

    x = x_ref[0]
    hb = _rms(x, g_ref[...]).astype(BF16)

    def proj(a, b):
        return jnp.dot(hb, win_ref[:, a:b], preferred_element_type=F32)

    xbuf[8:8 + T, :] = proj(E_XBC, E_Q)
    conv = cb_ref[...]
    for k in range(SSD_CONV):
        conv = conv + cw_ref[k:k + 1, :] * xbuf[5 + k:5 + k + T, :]
    tail = xbuf[T:T + 8, :]
    xbuf[0:8, :] = tail

    @pl.when(t == nt - 1)
    def _():
        nconv_ref[0] = tail[5:8, :]

    xc = _silu(conv)
    dt = _softplus(proj(E_DT, E_END) + dtb_ref[...])
    la = dt * (-jnp.exp(alog_ref[...]))

    lane = lax.broadcasted_iota(jnp.int32, (CHUNK, LANES), 1)
    row = lax.broadcasted_iota(jnp.int32, (CHUNK, LANES), 0)
    lo = lane < 64
    causal = row >= lane
    ltri = causal.astype(F32)

    q_all = _rotary(proj(E_Q, E_K), cos_ref[...], sin_ref[...], lane[:1])
    k_all = _rotary(proj(E_K, E_V), cos_ref[...], sin_ref[...], lane[:1]) * (RET_QK_DIM ** -0.5)
    v_all = proj(E_V, E_G)

    y_chunks, o_chunks = [], []
    for c in range(T // CHUNK):
        r0 = c * CHUNK
        rows = slice(r0, r0 + CHUNK)
        acum = jnp.dot(ltri, la[rows], precision=HIGHEST, preferred_element_type=F32)
        acum_t = acum.T
        tot = acum[CHUNK - 1:CHUNK, :]
        ea = jnp.exp(acum)
        wend = jnp.exp(tot - acum)
        etot = jnp.exp(tot)
        dt_c = dt[rows]
        y_pairs = []
        for g in range(SSD_GROUPS):
            b_g = xc[rows, 1024 + g * 128:1024 + (g + 1) * 128]
            c_g = xc[rows, 1280 + g * 128:1280 + (g + 1) * 128]
            gram = _bdot_t(c_g, b_g)
            y_inter = _bdot(c_g, sstate[g])
            xw, et = [], []
            for pp in range(4):
                p = g * 4 + pp
                xs_p = xc[rows, p * LANES:(p + 1) * LANES]
                xdt = xs_p * _pair_cols(dt_c, p, lo)
                s_pair = []
                for h in (2 * p, 2 * p + 1):
                    diff = acum[:, h:h + 1] - acum_t[h:h + 1, :]
                    dec = jnp.where(causal, jnp.exp(jnp.minimum(diff, 0.0)), 0.0)
                    s_pair.append((gram * dec).astype(BF16))
                rhs = jnp.concatenate([jnp.where(lo, xdt, 0.0), jnp.where(lo, 0.0, xdt)], axis=0)
                y_p = _bdot(jnp.concatenate(s_pair, axis=1), rhs)
                y_p = y_p + y_inter[:, pp * LANES:(pp + 1) * LANES] * _pair_cols(ea, p, lo)
                y_p = y_p + dsk_ref[:, p * LANES:(p + 1) * LANES] * xs_p
                y_pairs.append(y_p)
                xw.append(xdt * _pair_cols(wend, p, lo))
                et.append(_pair_cols(etot, p, lo[:1]))
            kv = _bdot(b_g.T, jnp.concatenate(xw, axis=1))
            sstate[g] = sstate[g] * jnp.concatenate(et, axis=1) + kv
        y_chunks.append(jnp.concatenate(y_pairs, axis=1))
        o_heads = []
        for p in range(RET_HEADS // 2):
            q_p = q_all[rows, p * LANES:(p + 1) * LANES]
            k_p = k_all[rows, p * LANES:(p + 1) * LANES]
            st_p = rstate[p * LANES:(p + 1) * LANES, :]
            ktw = k_p.T * rwe_ref[p]
            for hh in range(2):
                h = 2 * p + hh
                q_h = jnp.where(lo, q_p, 0.0) if hh == 0 else jnp.where(lo, 0.0, q_p)
                v_h = v_all[rows, h * LANES:(h + 1) * LANES]
                sc = (_bdot_t(q_h, k_p) * rd_ref[h]).astype(BF16)
                o_h = _bdot(sc, v_h) + _bdot(q_h, st_p) * rea_ref[h]
                o_heads.append(_rms_only(o_h))
                r = slice(h * 64, (h + 1) * 64)
                rstate[r, :] = rstate[r, :] * rgt_ref[r, :] + _bdot(ktw[hh * 64:(hh + 1) * 64, :], v_h)
        o_chunks.append(jnp.concatenate(o_heads, axis=1))

    y = jnp.concatenate(y_chunks, axis=0) if len(y_chunks) > 1 else y_chunks[0]
    o = jnp.concatenate(o_chunks, axis=0) if len(o_chunks) > 1 else o_chunks[0]
    y = y * _silu(proj(E_Z, E_XBC))
    y = jnp.concatenate([_rms_only(y[:, :512]), _rms_only(y[:, 512:])], axis=1) * gn_ref[...]
    o = o * _silu(proj(E_G, E_DT))
    mixed = jnp.concatenate([y, o], axis=1).astype(BF16)
    xo_ref[0] = x + jnp.dot(mixed, wout_ref[...], preferred_element_type=F32)

    @pl.when(t == nt - 1)
    def _():
        nssm_ref[0] = sstate[...]
        nret_ref[0] = rstate[...]


def _pad_lanes(v):
    return jnp.pad(v.astype(F32), [(0, 0)] * (v.ndim - 1) + [(0, LANES - v.shape[-1])])[..., None, :]


def _rows(v):
    return v.astype(F32)[:, None, :]


def _even_params(p):
    w = p['w_in_even']
    cols = [w[..., 0:1024], w[..., 1024:2560], w[..., 2576:3088], w[..., 3088:3600], w[..., 3600:4624],
            w[..., 4624:5648], w[..., 2560:2576], jnp.zeros(w.shape[:2] + (LANES - SSD_HEADS,), w.dtype)]
    return (jnp.concatenate(cols, axis=-1).astype(BF16), p['ssd_conv_w'], _rows(p['ssd_conv_b']),
            _pad_lanes(p['ssd_dt_bias']), _pad_lanes(p['ssd_a_log']),
            _rows(jnp.repeat(p['ssd_d'], SSD_HEAD_DIM, axis=1)), _rows(p['ssd_norm']), p['w_out_even'].astype(BF16))


def _even_layer_params(ep, i):
    return tuple(_Layer(a, i) for a in ep)


def _rope_tables(pos):
    half = RET_QK_DIM // 2
    inv = ROPE_BASE ** (-jnp.arange(half, dtype=F32) / half)
    ang = pos[:, None] * inv[None, :]
    cos, sin = jnp.cos(ang), jnp.sin(ang)
    cos_f = jnp.tile(jnp.concatenate([cos, cos], axis=1), (1, RET_HEADS))
    sin_s = jnp.tile(jnp.concatenate([-sin, sin], axis=1), (1, RET_HEADS))
    return cos_f, sin_s


def _ret_consts():
    lg = jnp.log1p(-jnp.exp2(-5.0 - jnp.arange(RET_HEADS, dtype=F32)))
    acum = jnp.cumsum(jnp.broadcast_to(lg[:, None], (RET_HEADS, CHUNK)), axis=1)
    idx = jnp.arange(CHUNK)
    diff = acum[:, :, None] - acum[:, None, :]
    rd = jnp.where((idx[:, None] >= idx[None, :])[None], jnp.exp(jnp.minimum(diff, 0.0)), 0.0)
    rea = jnp.broadcast_to(jnp.exp(acum)[:, :, None], (RET_HEADS, CHUNK, RET_V_DIM))
    total = acum[:, -1]
    we = jnp.exp(total[:, None] - acum)
    rwe = jnp.broadcast_to(we[:, None, :], (RET_HEADS, RET_QK_DIM, CHUNK)).reshape(RET_HEADS // 2, 128, CHUNK)
    rgt = jnp.broadcast_to(jnp.exp(total)[:, None, None], (RET_HEADS, RET_QK_DIM, RET_V_DIM)).reshape(512, RET_V_DIM)
    return rd, rea, rwe, rgt, lg


def _even_prompt(x, g, win, cw, cb, dtb, alog, dsk, gn, wout, cos_f, sin_s, rconsts, T=256):
    B, L, D = x.shape
    rd, rea, rwe, rgt, _ = rconsts
    grid = (B, L // T)
    tab = pl.BlockSpec((T, RET_QK_WIDTH), lambda b, t: (t, 0))
    in_specs = ([pl.BlockSpec((1, T, D), lambda b, t: (b, t, 0))] + _specs([g, win, cw, cb, dtb, alog, dsk, gn])
                + [tab, tab] + _specs([rd, rea, rwe, rgt, wout]))
    out_shape = [
        jax.ShapeDtypeStruct((B, L, D), F32),
        jax.ShapeDtypeStruct((B, SSD_CONV - 1, SSD_CONV_CH), F32),
        jax.ShapeDtypeStruct((B, SSD_GROUPS, SSD_STATE, 512), F32),
        jax.ShapeDtypeStruct((B, 512, RET_V_DIM), F32),
    ]
    out_specs = [
        pl.BlockSpec((1, T, D), lambda b, t: (b, t, 0)),
        pl.BlockSpec((1, SSD_CONV - 1, SSD_CONV_CH), lambda b, t: (b, 0, 0)),
        pl.BlockSpec((1, SSD_GROUPS, SSD_STATE, 512), lambda b, t: (b, 0, 0, 0)),
        pl.BlockSpec((1, 512, RET_V_DIM), lambda b, t: (b, 0, 0)),
    ]
    scratch = [pltpu.VMEM((T + 8, SSD_CONV_CH), F32), pltpu.VMEM((SSD_GROUPS, SSD_STATE, 512), F32),
               pltpu.VMEM((512, RET_V_DIM), F32)]
    xo, nconv, nssm, nret = pl.pallas_call(
        functools.partial(_even_prompt_kernel, T=T), grid=grid, in_specs=in_specs, out_specs=out_specs,
        out_shape=out_shape, scratch_shapes=scratch, compiler_params=_cparams("parallel", "arbitrary"),
        name="even_prompt",
    )(*_arrs([x, g, win, cw, cb, dtb, alog, dsk, gn, cos_f, sin_s, rd, rea, rwe, rgt, wout]))
    nssm = nssm.reshape(B, SSD_GROUPS, SSD_STATE, 8, SSD_HEAD_DIM).transpose(0, 1, 3, 2, 4)
    nssm = nssm.reshape(B, SSD_HEADS, SSD_STATE, SSD_HEAD_DIM)
    nret = nret.reshape(B, RET_HEADS, RET_QK_DIM, RET_V_DIM)
    return xo, nconv, nssm, nret


SB = 8


def _even_sample_kernel(x_ref, g_ref, win_ref, cw_ref, cb_ref, dtb_ref, alog_ref, dsk_ref, gn_ref,
                        cos_ref, sin_ref, gam_ref, wout_ref, cst_ref, sst_ref, rst_ref,
                        xo_ref, nconv_ref, nsst_ref, nrst_ref,
                        hb_s, xc_s, dt_s, e_s, q_s, k_s, v_s, y_s, o_s):
    i = pl.program_id(0)
    n = pl.num_programs(0)
    lane = lax.broadcasted_iota(jnp.int32, (1, LANES), 1)
    lo = lane < 64

    @pl.when(i == 0)
    def _():
        hb = _rms(x_ref[...], g_ref[...]).astype(BF16)
        hb_s[...] = hb

        def proj(a, b):
            return jnp.dot(hb, win_ref[:, a:b], preferred_element_type=F32)

        xbc = proj(E_XBC, E_Q)
        conv = cb_ref[...] + cw_ref[3:4, :] * xbc
        for k in range(SSD_CONV - 1):
            conv = conv + cw_ref[k:k + 1, :] * cst_ref[0, k]
        nconv_ref[0] = cst_ref[0, 1]
        nconv_ref[1] = cst_ref[0, 2]
        nconv_ref[2] = xbc
        xc_s[...] = _silu(conv)
        dt = _softplus(proj(E_DT, E_END) + dtb_ref[...])
        dt_s[...] = dt
        e_s[...] = jnp.exp(dt * (-jnp.exp(alog_ref[...])))
        q_s[...] = _rotary(proj(E_Q, E_K), cos_ref[...], sin_ref[...], lane)
        k_s[...] = _rotary(proj(E_K, E_V), cos_ref[...], sin_ref[...], lane) * (RET_QK_DIM ** -0.5)
        v_s[...] = proj(E_V, E_G)

    r = pl.ds(pl.multiple_of(i * SB, SB), SB)
    xc = xc_s[r, :]
    dt = dt_s[r, :]
    er = e_s[r, :]
    pad = jnp.zeros((LANES - 2 * SB, LANES), F32)
    lane_sq = lax.broadcasted_iota(jnp.int32, (LANES, LANES), 1)
    for g in range(SSD_GROUPS):
        b_g = xc[:, 1024 + g * 128:1024 + (g + 1) * 128]
        c_g = xc[:, 1280 + g * 128:1280 + (g + 1) * 128]
        cb_dot = jnp.sum(c_g * b_g, axis=-1, keepdims=True)
        for pp in range(4):
            c = g * 4 + pp
            xs_p = xc[:, c * LANES:(c + 1) * LANES]
            xdt = xs_p * _pair_cols(dt, c, lo)
            e_p = _pair_cols(er, c, lo)
            cols = jnp.concatenate([xdt, e_p, pad], axis=0).T
            ycols = jnp.zeros((LANES, LANES), F32)
            for j in range(SB):
                h2 = jnp.concatenate([sst_ref[0, j, 2 * c], sst_ref[0, j, 2 * c + 1]], axis=0)
                new = cols[:, SB + j:SB + j + 1] * h2 + cols[:, j:j + 1] * b_g[j:j + 1, :]
                nsst_ref[0, j, 2 * c] = new[:SSD_HEAD_DIM]
                nsst_ref[0, j, 2 * c + 1] = new[SSD_HEAD_DIM:]
                ycol = jnp.sum(h2 * c_g[j:j + 1, :], axis=-1, keepdims=True)
                ycols = jnp.where(lane_sq == j, ycol, ycols)
            y_p = cb_dot * xdt + e_p * ycols.T[0:SB, :] + dsk_ref[:, c * LANES:(c + 1) * LANES] * xs_p
            y_s[r, c * LANES:(c + 1) * LANES] = y_p
    q = q_s[r, :]
    k = k_s[r, :]
    v = v_s[r, :]
    for c in range(RET_HEADS // 2):
        q_p = q[:, c * LANES:(c + 1) * LANES]
        k_p = k[:, c * LANES:(c + 1) * LANES]
        prod = q_p * k_p
        cols = jnp.concatenate([q_p, k_p, pad], axis=0).T
        for hh in range(2):
            h = 2 * c + hh
            hr = slice(hh * RET_QK_DIM, (hh + 1) * RET_QK_DIM)
            keep = lo if hh == 0 else jnp.logical_not(lo)
            qk = jnp.sum(jnp.where(keep, prod, 0.0), axis=-1, keepdims=True)
            v_h = v[:, h * LANES:(h + 1) * LANES]
            gam = gam_ref[:, h:h + 1]
            rows = []
            for j in range(SB):
                hst = rst_ref[0, j, h]
                nrst_ref[0, j, h] = gam * hst + cols[hr, SB + j:SB + j + 1] * v_h[j:j + 1, :]
                rows.append(jnp.sum(cols[hr, j:j + 1] * hst, axis=0, keepdims=True))
            o_s[r, h * LANES:(h + 1) * LANES] = _rms_only(qk * v_h + gam * jnp.concatenate(rows, axis=0))

    @pl.when(i == n - 1)
    def _():
        hb = hb_s[...]
        y = y_s[...] * _silu(jnp.dot(hb, win_ref[:, E_Z:E_XBC], preferred_element_type=F32))
        y = jnp.concatenate([_rms_only(y[:, :512]), _rms_only(y[:, 512:])], axis=1) * gn_ref[...]
        o = o_s[...] * _silu(jnp.dot(hb, win_ref[:, E_G:E_DT], preferred_element_type=F32))
        mixed = jnp.concatenate([y, o], axis=1).astype(BF16)
        xo_ref[...] = x_ref[...] + jnp.dot(mixed, wout_ref[...], preferred_element_type=F32)


def _even_sample_kernel_chained(*refs):
    _even_sample_kernel(*refs[:16], *refs[18:])


def _even_sample(x, g, win, cw, cb, dtb, alog, dsk, gn, wout, cos_f, sin_s, gam, conv_all, ssm_all, ret_all, layer,
                 prev=None):
    S, D = x.shape
    cshape = (SSD_CONV - 1, S, SSD_CONV_CH)
    sblk = lambda a: pl.BlockSpec((1, SB) + a.shape[2:], lambda i: (layer, i, 0, 0, 0))
    ops = [x, g, win, cw, cb, dtb, alog, dsk, gn, cos_f, sin_s, gam, wout]
    in_specs = _specs(ops) + [pl.BlockSpec((1,) + cshape, lambda i: (layer, 0, 0, 0)), sblk(ssm_all), sblk(ret_all)]
    args = _arrs(ops) + [conv_all, ssm_all, ret_all]
    body, aliases = _even_sample_kernel, {}
    if prev is not None:
        body, aliases = _even_sample_kernel_chained, {len(args): 2, len(args) + 1: 3}
        in_specs = in_specs + [pl.BlockSpec(memory_space=pl.ANY)] * 2
        args = args + list(prev)
    out_shape = [jax.ShapeDtypeStruct(sh, F32) for sh in (x.shape, cshape, ssm_all.shape, ret_all.shape)]
    out_specs = [_full(x.shape), _full(cshape), sblk(ssm_all), sblk(ret_all)]
    scratch = [pltpu.VMEM((S, D), BF16), pltpu.VMEM((S, SSD_CONV_CH), F32), pltpu.VMEM((S, LANES), F32),
               pltpu.VMEM((S, LANES), F32), pltpu.VMEM((S, RET_QK_WIDTH), F32), pltpu.VMEM((S, RET_QK_WIDTH), F32),
               pltpu.VMEM((S, RET_V_WIDTH), F32), pltpu.VMEM((S, D), F32), pltpu.VMEM((S, RET_V_WIDTH), F32)]
    return pl.pallas_call(
        body, grid=(S // SB,), in_specs=in_specs, out_specs=out_specs, out_shape=out_shape,
        scratch_shapes=scratch, input_output_aliases=aliases, compiler_params=_cparams("arbitrary"),
        name="even_sample",
    )(*args)


def _layernorm_silu(c, g, b):
    mu = jnp.mean(c, axis=-1, keepdims=True)
    var = jnp.mean(jnp.square(c - mu), axis=-1, keepdims=True)
    return _silu((c - mu) * lax.rsqrt(var + EPS) * g + b)


def _pool_project(parts, pw_ref, ps_ref):
    proj = [_bdot(parts[gi], pw_ref[gi]) for gi in range(len(POOL_WINDOWS))]
    return jnp.concatenate(proj, axis=1) * ps_ref[...]


def _odd_prompt_kernel(x_ref, g_ref, win_ref, pw_ref, ps_ref, cw_ref, cb_ref, cng_ref, cnb_ref, wout_ref,
                       xo_ref, npool_ref, nconv_ref, ubuf, gbuf, sbuf, *, T):
    t = pl.program_id(1)
    nt = pl.num_programs(1)

    @pl.when(t == 0)
    def _():
        ubuf[0:16, :] = jnp.zeros((16, POOL_WIDTH), F32)
        gbuf[0:32, :] = jnp.zeros((32, CONV_WIDTH), F32)

    x = x_ref[0]
    hb = _rms(x, g_ref[...]).astype(BF16)
    u = jnp.dot(hb, win_ref[:, 0:512], preferred_element_type=F32)
    ga = jnp.dot(hb, win_ref[:, 512:1024], preferred_element_type=F32)
    gb = jnp.dot(hb, win_ref[:, 1024:1536], preferred_element_type=F32)
    ubuf[16:16 + T, :] = u
    pos1 = (lax.broadcasted_iota(jnp.int32, (T, 1), 0) + t * T + 1).astype(F32)
    parts = []
    for gi, w in enumerate(POOL_WINDOWS):
        sl = slice(gi * POOL_GROUP_DIM, (gi + 1) * POOL_GROUP_DIM)
        s = u[:, sl]
        for k in range(1, w):
            s = s + ubuf[16 - k:16 - k + T, sl]
        parts.append(s / jnp.minimum(pos1, float(w)) - u[:, sl])
    pooled = _pool_project(parts, pw_ref, ps_ref)
    utail = ubuf[T:T + 16, :]
    ubuf[0:16, :] = utail
    glu = ga * jax.nn.sigmoid(gb)
    gbuf[32:32 + T, :] = glu
    c = cb_ref[...] + cw_ref[CONV_K - 1:CONV_K, :] * glu
    for res in range(8):
        src_ref = gbuf
        if res:
            sbuf[...] = gbuf[res:res + T + 24, :]
            src_ref = sbuf
        for q in range(4):
            k = 8 * q + res - 2
            if 0 <= k < CONV_K - 1:
                c = c + cw_ref[k:k + 1, :] * src_ref[8 * q:8 * q + T, :]
    gtail = gbuf[T:T + 32, :]
    gbuf[0:32, :] = gtail
    c = _layernorm_silu(c, cng_ref[...], cnb_ref[...])
    mixed = jnp.concatenate([pooled, c], axis=1).astype(BF16)
    xo_ref[0] = x + jnp.dot(mixed, wout_ref[...], preferred_element_type=F32)

    @pl.when(t == nt - 1)
    def _():
        npool_ref[0] = utail[1:16, :]
        nconv_ref[0] = gtail[2:32, :]


def _odd_params(p):
    return (p['w_in_odd'].astype(BF16), p['pool_w'].astype(BF16), _rows(p['pool_scale']), p['conv_w'],
            _rows(p['conv_b']), _rows(p['conv_norm_g']), _rows(p['conv_norm_b']), p['w_out_odd'].astype(BF16))


def _odd_layer_params(op, i):
    return tuple(_Layer(a, i) for a in op)


def _odd_prompt(x, g, win, pw, ps, cw, cb, cng, cnb, wout, T=256):
    B, L, D = x.shape
    consts = (g, win, pw, ps, cw, cb, cng, cnb, wout)
    in_specs = [pl.BlockSpec((1, T, D), lambda b, t: (b, t, 0))] + _specs(consts)
    out_shape = [jax.ShapeDtypeStruct((B, L, D), F32), jax.ShapeDtypeStruct((B, POOL_STATE, POOL_WIDTH), F32),
                 jax.ShapeDtypeStruct((B, CONV_K - 1, CONV_WIDTH), F32)]
    out_specs = [pl.BlockSpec((1, T, D), lambda b, t: (b, t, 0)),
                 pl.BlockSpec((1, POOL_STATE, POOL_WIDTH), lambda b, t: (b, 0, 0)),
                 pl.BlockSpec((1, CONV_K - 1, CONV_WIDTH), lambda b, t: (b, 0, 0))]
    scratch = [pltpu.VMEM((T + 16, POOL_WIDTH), F32), pltpu.VMEM((T + 32, CONV_WIDTH), F32),
               pltpu.VMEM((T + 24, CONV_WIDTH), F32)]
    return pl.pallas_call(
        functools.partial(_odd_prompt_kernel, T=T), grid=(B, L // T), in_specs=in_specs, out_specs=out_specs,
        out_shape=out_shape, scratch_shapes=scratch, compiler_params=_cparams("parallel", "arbitrary"),
        name="odd_prompt",
    )(x, *_arrs(consts))


def _odd_sample_kernel(x_ref, g_ref, win_ref, pw_ref, ps_ref, cw_ref, cb_ref, cng_ref, cnb_ref, wout_ref,
                       pst_ref, cst_ref, xo_ref, npool_ref, nconv_ref):
    x = x_ref[...]
    hb = _rms(x, g_ref[...]).astype(BF16)
    u = jnp.dot(hb, win_ref[:, 0:512], preferred_element_type=F32)
    ga = jnp.dot(hb, win_ref[:, 512:1024], preferred_element_type=F32)
    gb = jnp.dot(hb, win_ref[:, 1024:1536], preferred_element_type=F32)
    parts = []
    for gi, w in enumerate(POOL_WINDOWS):
        lo_, hi_ = gi * POOL_GROUP_DIM, (gi + 1) * POOL_GROUP_DIM
        s = u[:, lo_:hi_]
        for k in range(1, w):
            s = s + pst_ref[0, POOL_STATE - k, :, lo_:hi_]
        parts.append(s / float(w) - u[:, lo_:hi_])
    pooled = _pool_project(parts, pw_ref, ps_ref)
    for j in range(POOL_STATE - 1):
        npool_ref[j] = pst_ref[0, j + 1]
    npool_ref[POOL_STATE - 1] = u
    glu = ga * jax.nn.sigmoid(gb)
    c = cb_ref[...] + cw_ref[CONV_K - 1:CONV_K, :] * glu
    for k in range(CONV_K - 1):
        c = c + cw_ref[k:k + 1, :] * cst_ref[0, k]
    for j in range(CONV_K - 2):
        nconv_ref[j] = cst_ref[0, j + 1]
    nconv_ref[CONV_K - 2] = glu
    c = _layernorm_silu(c, cng_ref[...], cnb_ref[...])
    mixed = jnp.concatenate([pooled, c], axis=1).astype(BF16)
    xo_ref[...] = x + jnp.dot(mixed, wout_ref[...], preferred_element_type=F32)


def _odd_sample(x, g, win, pw, ps, cw, cb, cng, cnb, wout, pool_all, conv_all, layer):
    S, D = x.shape
    pshape = (POOL_STATE, S, POOL_WIDTH)
    cshape = (CONV_K - 1, S, CONV_WIDTH)
    consts = (x, g, win, pw, ps, cw, cb, cng, cnb, wout)
    lspec = lambda sh: pl.BlockSpec((1,) + sh, lambda i: (layer, 0, 0, 0))
    outs = (x.shape, pshape, cshape)
    return pl.pallas_call(
        _odd_sample_kernel, grid=(1,), in_specs=_specs(consts) + [lspec(pshape), lspec(cshape)],
        out_specs=[_full(sh) for sh in outs], out_shape=[jax.ShapeDtypeStruct(sh, F32) for sh in outs],
        compiler_params=_cparams("arbitrary"), name="odd_sample",
    )(*_arrs(consts), pool_all, conv_all)


def _mem_kv_kernel(m_ref, g_ref, w_ref, k_ref, v_ref):
    hb = _rms(m_ref[...], g_ref[0]).astype(BF16)
    kv = jnp.dot(hb, w_ref[0], preferred_element_type=F32)
    k_ref[0] = kv[:, :D_MODEL]
    v_ref[0] = kv[:, D_MODEL:]


def _mem_kv(mem, norm_mem, wkv, tm=512):
    M, D = mem.shape
    out = jax.ShapeDtypeStruct((DEPTH, M, D), F32)
    return pl.pallas_call(
        _mem_kv_kernel, grid=(DEPTH, M // tm),
        in_specs=[pl.BlockSpec((tm, D), lambda l, m: (m, 0)), pl.BlockSpec((1, 1, D), lambda l, m: (l, 0, 0)),
                  pl.BlockSpec((1, D, 2 * D), lambda l, m: (l, 0, 0))],
        out_specs=[pl.BlockSpec((1, tm, D), lambda l, m: (l, m, 0))] * 2, out_shape=[out, out],
        compiler_params=_cparams("parallel", "parallel"), name="mem_kv",
    )(mem, norm_mem[:, None, :], wkv)


def _xattn_prompt_kernel(x_ref, g_ref, wq_ref, k_ref, v_ref, wo_ref, xo_ref):
    x = x_ref[0]
    hb = _rms(x, g_ref[...]).astype(BF16)
    q = jnp.dot(hb, wq_ref[...], preferred_element_type=F32) * (X_HEAD_DIM ** -0.5)
    outs = []
    for h in range(X_HEADS):
        sl = slice(h * X_HEAD_DIM, (h + 1) * X_HEAD_DIM)
        s = _bdot_t(q[:, sl], k_ref[0, :, sl])
        p = jnp.exp(s - jnp.max(s, axis=-1, keepdims=True))
        outs.append(_bdot(p, v_ref[0, :, sl]) / jnp.sum(p, axis=-1, keepdims=True))
    o = jnp.concatenate(outs, axis=1).astype(BF16)
    xo_ref[0] = x + jnp.dot(o, wo_ref[...], preferred_element_type=F32)


def _xattn_prompt(x, g, wq, mk, mv, wo, l, T=512):
    B, L, D = x.shape
    xspec = pl.BlockSpec((1, T, D), lambda b, t: (b, t, 0))
    mspec = pl.BlockSpec((None, 1, MEM_LEN, D), lambda b, t: (l, b, 0, 0))
    return pl.pallas_call(
        _xattn_prompt_kernel, grid=(B, L // T),
        in_specs=[xspec] + _specs([g, wq]) + [mspec, mspec] + _specs([wo]),
        out_specs=xspec, out_shape=jax.ShapeDtypeStruct(x.shape, F32),
        compiler_params=_cparams("parallel", "parallel"), name="xattn_prompt",
    )(*_arrs([x, g, wq, mk, mv, wo]))


XB = 4


def _xattn_sample_kernel(x_ref, g_ref, wq_ref, k_ref, v_ref, wo_ref, xo_ref, q_s, o_s):
    i = pl.program_id(0)
    n = pl.num_programs(0)

    @pl.when(i == 0)
    def _():
        hb = _rms(x_ref[...], g_ref[...]).astype(BF16)
        q_s[...] = jnp.dot(hb, wq_ref[...], preferred_element_type=F32) * (X_HEAD_DIM ** -0.5)

    for j in range(XB):
        r = pl.ds(i * XB + j, 1)
        qrow = q_s[r, :]
        q8 = jnp.concatenate([qrow[:, c * LANES:(c + 1) * LANES] for c in range(8)], axis=0)
        part = jnp.sum(k_ref[0, j] * q8[None], axis=-1, keepdims=True)
        s = part + pltpu.roll(part, 4, 1)
        p = jnp.exp(s - jnp.max(s, axis=0, keepdims=True))
        o8 = jnp.sum(p * v_ref[0, j], axis=0) / jnp.sum(p, axis=0)
        o_s[r, :] = jnp.concatenate([o8[c:c + 1, :] for c in range(8)], axis=1)

    @pl.when(i == n - 1)
    def _():
        xo_ref[...] = x_ref[...] + jnp.dot(o_s[...].astype(BF16), wo_ref[...], preferred_element_type=F32)


def _cache_tiles(c):
    d, s, m = c.shape[:3]
    return c.reshape(d, s, m, X_HEADS, 2, LANES).transpose(0, 1, 2, 4, 3, 5).reshape(d, s, m, 8, LANES)


def _xattn_perm_weights(wq, wo):
    n, d = wq.shape[:2]
    wq_p = wq.reshape(n, d, X_HEADS, 2, LANES).transpose(0, 1, 3, 2, 4).reshape(n, d, d)
    wo_p = wo.reshape(n, X_HEADS, 2, LANES, d).transpose(0, 2, 1, 3, 4).reshape(n, d, d)
    return wq_p, wo_p


def _xattn_sample(x, g, wq_p, ck, cv, wo_p, l):
    S, D = x.shape
    cspec = pl.BlockSpec((1, XB, MEM_LEN, 8, LANES), lambda i: (l, i, 0, 0, 0))
    return pl.pallas_call(
        _xattn_sample_kernel, grid=(S // XB,),
        in_specs=_specs([x, g, wq_p]) + [cspec, cspec] + _specs([wo_p]),
        out_specs=_full(x.shape), out_shape=jax.ShapeDtypeStruct(x.shape, F32),
        scratch_shapes=[pltpu.VMEM((S, D), F32), pltpu.VMEM((S, D), F32)],
        compiler_params=_cparams("arbitrary"), name="xattn_sample",
    )(*_arrs([x, g, wq_p, ck, cv, wo_p]))


def _ffn_kernel(x_ref, g_ref, w1_ref, w3_ref, w2_ref, xo_ref):
    x = x_ref[...]
    hb = _rms(x, g_ref[...]).astype(BF16)
    a = jnp.dot(hb, w1_ref[...], preferred_element_type=F32)
    b = jnp.dot(hb, w3_ref[...], preferred_element_type=F32)
    xo_ref[...] = x + jnp.dot((_silu(a) * b).astype(BF16), w2_ref[...], preferred_element_type=F32)


def _ffn(x, g, w1, w3, w2, tm):
    M, D = x.shape
    xspec = pl.BlockSpec((tm, D), lambda m: (m, 0))
    return pl.pallas_call(
        _ffn_kernel, grid=(M // tm,),
        in_specs=[xspec] + _specs([g, w1, w3, w2]),
        out_specs=xspec, out_shape=jax.ShapeDtypeStruct(x.shape, F32),
        compiler_params=_cparams("parallel"), name="ffn",
    )(*_arrs([x, g, w1, w3, w2]))


def _router_logits(h, rw_ref, rb_ref):
    h_hi = h.astype(BF16)
    h_lo = (h - h_hi.astype(F32)).astype(BF16)
    a = jnp.dot(h_hi, rw_ref[...], preferred_element_type=F32)
    b = jnp.dot(h_lo, rw_ref[:, :LANES], preferred_element_type=F32)
    return a[:, :LANES] + a[:, LANES:] + b + rb_ref[...]


def _moe_kernel(x_ref, g_ref, rw_ref, rb_ref, w1_ref, w3_ref, w2_ref, xo_ref, hb_s, comb_s, acc_s):
    e = pl.program_id(1)
    lane = lax.broadcasted_iota(jnp.int32, (1, LANES), 1)

    @pl.when(e == 0)
    def _():
        h = _rms(x_ref[...], g_ref[...])
        hb_s[...] = h.astype(BF16)
        logits = _router_logits(h, rw_ref, rb_ref)
        m1 = jnp.max(logits, axis=-1, keepdims=True)
        i1 = jnp.min(jnp.where(logits == m1, lane, LANES), axis=-1, keepdims=True)
        rest = jnp.where(lane == i1, -jnp.inf, logits)
        m2 = jnp.max(rest, axis=-1, keepdims=True)
        i2 = jnp.min(jnp.where(rest == m2, lane, LANES), axis=-1, keepdims=True)
        e2 = jnp.exp(m2 - m1)
        comb_s[...] = jnp.where(lane == i1, 1.0 / (1.0 + e2), 0.0) + jnp.where(lane == i2, e2 / (1.0 + e2), 0.0)
        acc_s[...] = jnp.zeros_like(acc_s)

    hb = hb_s[...]
    a = jnp.dot(hb, w1_ref[0], preferred_element_type=F32)
    b = jnp.dot(hb, w3_ref[0], preferred_element_type=F32)
    y = jnp.dot((_silu(a) * b).astype(BF16), w2_ref[0], preferred_element_type=F32)
    gate = jnp.sum(jnp.where(lane == e, comb_s[...], 0.0), axis=-1, keepdims=True)
    acc_s[...] += gate * y

    @pl.when(e == pl.num_programs(1) - 1)
    def _():
        xo_ref[...] = x_ref[...] + acc_s[...]


def _moe_params(p):
    rw = jnp.pad(p['router_w'].astype(F32), ((0, 0), (0, 0), (0, LANES - N_EXPERTS)))
    rw_hi = rw.astype(BF16)
    rw_lo = (rw - rw_hi.astype(F32)).astype(BF16)
    rb = jnp.pad(p['router_b'].astype(F32), ((0, 0), (0, LANES - N_EXPERTS)), constant_values=-1e30)[:, None, :]
    return (jnp.concatenate([rw_hi, rw_lo], axis=-1), rb,
            p['moe_w1'].astype(BF16), p['moe_w3'].astype(BF16), p['moe_w2'].astype(BF16))


def _moe_layer_params(mp, i):
    return tuple(_Layer(a, i) for a in mp)


def _expert_spec(w, pick):
    layer = w.idx
    return pl.BlockSpec((None, 1) + w.arr.shape[2:], lambda *a: (layer, pick(*a), 0, 0))


def _moe(x, g, rw, rb, w1, w3, w2, tm):
    M, D = x.shape
    xspec = pl.BlockSpec((tm, D), lambda m, e: (m, 0))
    pick = lambda m, e: e
    return pl.pallas_call(
        _moe_kernel, grid=(M // tm, N_EXPERTS),
        in_specs=[xspec] + _specs([g, rw, rb]) + [_expert_spec(w, pick) for w in (w1, w3, w2)],
        out_specs=xspec, out_shape=jax.ShapeDtypeStruct(x.shape, F32),
        scratch_shapes=[pltpu.VMEM((tm, D), BF16), pltpu.VMEM((tm, LANES), F32), pltpu.VMEM((tm, D), F32)],
        compiler_params=_cparams("parallel", "arbitrary"), name="moe",
    )(*_arrs([x, g, rw, rb, w1, w3, w2]))


MOE_TM = 512


def _moe_route_kernel(x_ref, g_ref, rw_ref, rb_ref, meta_ref, cnt_ref, base_s, *, tm):
    @pl.when(pl.program_id(0) == 0)
    def _():
        base_s[...] = jnp.zeros_like(base_s)

    lane = lax.broadcasted_iota(jnp.int32, (1, LANES), 1)
    h = _rms(x_ref[...], g_ref[...])
    logits = _router_logits(h, rw_ref, rb_ref)
    m1 = jnp.max(logits, axis=-1, keepdims=True)
    i1 = jnp.min(jnp.where(logits == m1, lane, LANES), axis=-1, keepdims=True)
    rest = jnp.where(lane == i1, -jnp.inf, logits)
    m2 = jnp.max(rest, axis=-1, keepdims=True)
    i2 = jnp.min(jnp.where(rest == m2, lane, LANES), axis=-1, keepdims=True)
    e2 = jnp.exp(m2 - m1)
    oh1 = lane == i1
    oh2 = lane == i2
    oh = jnp.logical_or(oh1, oh2).astype(F32)
    strict = (lax.broadcasted_iota(jnp.int32, (tm, tm), 0) > lax.broadcasted_iota(jnp.int32, (tm, tm), 1))
    pos = jnp.dot(strict.astype(BF16), oh.astype(BF16), preferred_element_type=F32) + base_s[...]
    r1 = jnp.sum(jnp.where(oh1, pos, 0.0), axis=-1, keepdims=True)
    r2 = jnp.sum(jnp.where(oh2, pos, 0.0), axis=-1, keepdims=True)
    base_s[...] += jnp.sum(oh, axis=0, keepdims=True)
    cols = (i1.astype(F32), i2.astype(F32), r1, r2, 1.0 / (1.0 + e2), e2 / (1.0 + e2))
    meta = jnp.zeros((tm, LANES), F32)
    for c, val in enumerate(cols):
        meta = jnp.where(lane == c, val, meta)
    meta_ref[...] = meta
    cnt_ref[...] = base_s[...]


def _moe_dispatch_kernel(dest_ref, x_ref, g_ref, xg_ref, hbuf, sem, *, tm, M):
    base = pl.program_id(0) * tm
    hbuf[...] = _rms(x_ref[...], g_ref[...])

    def body(r, c):
        for k in range(2):
            d = dest_ref[k * M + base + r]
            pltpu.make_async_copy(hbuf.at[pl.ds(r, 1)], xg_ref.at[pl.ds(d, 1)], sem).start(priority=k)
        return c

    lax.fori_loop(0, tm, body, 0, unroll=8)
    for k in range(2):
        pltpu.make_async_copy(hbuf, xg_ref.at[pl.ds(0, tm)], sem).wait()


def _moe_group_kernel(te_ref, tb_ref, nv_ref, tot_ref, xg_ref, w1_ref, w3_ref, w2_ref, y_ref):
    s = pl.program_id(0)

    @pl.when(s < tot_ref[0])
    def _():
        rows = lax.broadcasted_iota(jnp.int32, (MOE_TM, 1), 0)
        xt = jnp.where(rows < nv_ref[s], xg_ref[...], 0.0).astype(BF16)
        a = jnp.dot(xt, w1_ref[0], preferred_element_type=F32)
        b = jnp.dot(xt, w3_ref[0], preferred_element_type=F32)
        y_ref[...] = jnp.dot((_silu(a) * b).astype(BF16), w2_ref[0], preferred_element_type=F32)


def _moe_combine_kernel(dest_ref, x_ref, meta_ref, gfin_ref, y_ref, o_ref, ybuf, sem, *, tm, M, final):
    base = pl.program_id(0) * tm

    def body(r, c):
        for k in range(2):
            d = dest_ref[k * M + base + r]
            pltpu.make_async_copy(y_ref.at[pl.ds(d, 1)], ybuf.at[k, pl.ds(r, 1)], sem).start(priority=k)
        return c

    lax.fori_loop(0, tm, body, 0, unroll=8)
    for k in range(2):
        pltpu.make_async_copy(y_ref.at[pl.ds(0, tm)], ybuf.at[k], sem).wait()
    out = x_ref[...] + (meta_ref[:, 4:5] * ybuf[0] + meta_ref[:, 5:6] * ybuf[1])
    o_ref[...] = _rms(out, gfin_ref[...]) if final else out


def _moe_routed(x, g, rw, rb, w1, w3, w2, gfin=None, tm=512, tmd=256):
    M, D = x.shape
    TM = MOE_TM
    NT = 2 * M // TM + N_EXPERTS
    R = NT * TM
    xspec = pl.BlockSpec((tm, D), lambda m: (m, 0))
    meta, cnt = pl.pallas_call(
        functools.partial(_moe_route_kernel, tm=tm), grid=(M // tm,),
        in_specs=[xspec] + _specs([g, rw, rb]),
        out_specs=[pl.BlockSpec((tm, LANES), lambda m: (m, 0)), _full((1, LANES))],
        out_shape=[jax.ShapeDtypeStruct((M, LANES), F32), jax.ShapeDtypeStruct((1, LANES), F32)],
        scratch_shapes=[pltpu.VMEM((1, LANES), F32)], compiler_params=_cparams("arbitrary"), name="moe_route",
    )(*_arrs([x, g, rw, rb]))
    cnt = cnt[0, :N_EXPERTS].astype(jnp.int32)
    ntile = (cnt + TM - 1) // TM
    tile_end = jnp.cumsum(ntile)
    tile_start = tile_end - ntile
    total = tile_end[-1]
    s = jnp.arange(NT, dtype=jnp.int32)
    tb = jnp.maximum(jnp.minimum(s, total - 1), 0)
    te = jnp.minimum(jnp.sum((tb[:, None] >= tile_end[None, :]).astype(jnp.int32), axis=1), N_EXPERTS - 1)
    nv = jnp.clip(cnt[te] - (tb - tile_start[te]) * TM, 0, TM).astype(jnp.int32)
    row_off = (tile_start * TM).astype(jnp.int32)
    dest = jnp.concatenate([row_off[meta[:, 0].astype(jnp.int32)] + meta[:, 2].astype(jnp.int32),
                            row_off[meta[:, 1].astype(jnp.int32)] + meta[:, 3].astype(jnp.int32)])
    xdspec = pl.BlockSpec((tmd, D), lambda m, *_: (m, 0))
    xg = pl.pallas_call(
        functools.partial(_moe_dispatch_kernel, tm=tmd, M=M),
        grid_spec=pltpu.PrefetchScalarGridSpec(
            num_scalar_prefetch=1, grid=(M // tmd,),
            in_specs=[xdspec, g.spec],
            out_specs=pl.BlockSpec(memory_space=pl.ANY),
            scratch_shapes=[pltpu.VMEM((tmd, D), F32), pltpu.SemaphoreType.DMA(())]),
        out_shape=jax.ShapeDtypeStruct((R, D), F32), compiler_params=_cparams("arbitrary"), name="moe_dispatch",
    )(dest, x, g.arr)
    pick = lambda s, te, tb, nv, tot: te[s]
    tspec = pl.BlockSpec((TM, D), lambda s, te, tb, nv, tot: (tb[s], 0))
    y = pl.pallas_call(
        _moe_group_kernel,
        grid_spec=pltpu.PrefetchScalarGridSpec(
            num_scalar_prefetch=4, grid=(NT,),
            in_specs=[tspec] + [_expert_spec(w, pick) for w in (w1, w3, w2)],
            out_specs=tspec),
        out_shape=jax.ShapeDtypeStruct((R, D), F32), compiler_params=_cparams("arbitrary"), name="moe_group",
    )(te, tb, nv, total.reshape(1).astype(jnp.int32), xg, w1.arr, w3.arr, w2.arr)
    final = gfin is not None
    gfin = g if gfin is None else gfin
    return pl.pallas_call(
        functools.partial(_moe_combine_kernel, tm=tmd, M=M, final=final),
        grid_spec=pltpu.PrefetchScalarGridSpec(
            num_scalar_prefetch=1, grid=(M // tmd,),
            in_specs=[xdspec, pl.BlockSpec((tmd, LANES), lambda m, *_: (m, 0)), gfin.spec,
                      pl.BlockSpec(memory_space=pl.ANY)],
            out_specs=xdspec,
            scratch_shapes=[pltpu.VMEM((2, tmd, D), F32), pltpu.SemaphoreType.DMA(())]),
        out_shape=jax.ShapeDtypeStruct((M, D), F32), compiler_params=_cparams("arbitrary"), name="moe_combine",
    )(dest, x, meta, gfin.arr, y)


def _final_norm_kernel(x_ref, g_ref, o_ref):
    o_ref[...] = _rms(x_ref[...], g_ref[...])


def _final_norm(x, g, tm):
    M, D = x.shape
    spec = pl.BlockSpec((tm, D), lambda m: (m, 0))
    return pl.pallas_call(
        _final_norm_kernel, grid=(M // tm,), in_specs=[spec] + _specs([g]), out_specs=spec,
        out_shape=jax.ShapeDtypeStruct(x.shape, F32), compiler_params=_cparams("parallel"), name="final_norm",
    )(*_arrs([x, g]))


def kernel(x_prompt, x_sample, state_ssm, state_ssm_conv, state_ret, state_pool, state_conv, cache_mem_k, cache_mem_v, mem_prompt, norm_mix, norm_cross, norm_ffn, norm_mem, norm_final, w_in_even, ssd_conv_w, ssd_conv_b, ssd_dt_bias, ssd_a_log, ssd_d, ssd_norm, w_out_even, ffn_w1, ffn_w3, ffn_w2, w_in_odd, pool_w, pool_scale, conv_w, conv_b, conv_norm_g, conv_norm_b, w_out_odd, router_w, router_b, moe_w1, moe_w3, moe_w2, xattn_q, xattn_k, xattn_v, xattn_o):
    p = dict(norm_mix=norm_mix, norm_cross=norm_cross, norm_ffn=norm_ffn, norm_mem=norm_mem,
             w_in_even=w_in_even, ssd_conv_w=ssd_conv_w, ssd_conv_b=ssd_conv_b, ssd_dt_bias=ssd_dt_bias,
             ssd_a_log=ssd_a_log, ssd_d=ssd_d, ssd_norm=ssd_norm, w_out_even=w_out_even,
             ffn_w1=ffn_w1, ffn_w3=ffn_w3, ffn_w2=ffn_w2,
             w_in_odd=w_in_odd, pool_w=pool_w, pool_scale=pool_scale, conv_w=conv_w, conv_b=conv_b,
             conv_norm_g=conv_norm_g, conv_norm_b=conv_norm_b, w_out_odd=w_out_odd,
             router_w=router_w, router_b=router_b, moe_w1=moe_w1, moe_w3=moe_w3, moe_w2=moe_w2,
             xattn_q=xattn_q, xattn_k=xattn_k, xattn_v=xattn_v, xattn_o=xattn_o)
    B, L, D = x_prompt.shape
    S = x_sample.shape[0]
    rconsts = _ret_consts()
    cos_p, sin_p = _rope_tables(jnp.arange(L, dtype=F32))
    cos_s, sin_s = _rope_tables(PAST_LEN + jnp.arange(1, dtype=F32))
    gam = _pad_lanes(jnp.exp(rconsts[4]))

    mem_k, mem_v = _mem_kv(mem_prompt.reshape(B * MEM_LEN, D), norm_mem,
                           jnp.concatenate([xattn_k, xattn_v], axis=2).astype(BF16))
    mem_k = mem_k.reshape(DEPTH, B, MEM_LEN, D)
    mem_v = mem_v.reshape(DEPTH, B, MEM_LEN, D)
    cache_k = _cache_tiles(cache_mem_k)
    cache_v = _cache_tiles(cache_mem_v)
    sconv_rows = state_ssm_conv.transpose(0, 2, 1, 3)
    ssm_rows = state_ssm.swapaxes(3, 4)
    chain = None
    pool_rows = state_pool.transpose(0, 2, 1, 3)
    conv_rows = state_conv.transpose(0, 2, 1, 3)

    xp = x_prompt
    xs = x_sample.reshape(S, D)
    outs_p = dict(ssm=[], ssm_conv=[], ret=[], pool=[], conv=[])
    outs_s = dict(ssm=[], ssm_conv=[], ret=[], pool=[], conv=[])
    ep, op, mp = _even_params(p), _odd_params(p), _moe_params(p)
    fp = (ffn_w1.astype(BF16), ffn_w3.astype(BF16), ffn_w2.astype(BF16))
    wq_all, wo_all = xattn_q.astype(BF16), xattn_o.astype(BF16)
    wq_perm, wo_perm = _xattn_perm_weights(wq_all, wo_all)
    n_mix, n_cross, n_ffn = _rows(norm_mix), _rows(norm_cross), _rows(norm_ffn)
    g_fin = _Layer(_rows(norm_final[None]), 0)
    for l in range(DEPTH):
        i = l // 2
        g_mix = _Layer(n_mix, l)
        if l % 2 == 0:
            ew = _even_layer_params(ep, i)
            xp, c_st, s_st, r_st = _even_prompt(xp, g_mix, *ew, cos_p, sin_p, rconsts)
            outs_p['ssm_conv'].append(c_st); outs_p['ssm'].append(s_st); outs_p['ret'].append(r_st)
            xs, c_st, *chain = _even_sample(xs, g_mix, *ew, cos_s, sin_s, gam,
                                            sconv_rows, ssm_rows, state_ret, i, chain)
            outs_s['ssm_conv'].append(c_st)
        else:
            ow = _odd_layer_params(op, i)
            xp, p_st, c_st = _odd_prompt(xp, g_mix, *ow)
            outs_p['pool'].append(p_st); outs_p['conv'].append(c_st)
            xs, p_st, c_st = _odd_sample(xs, g_mix, *ow, pool_rows, conv_rows, i)
            outs_s['pool'].append(p_st); outs_s['conv'].append(c_st)
        g_x = _Layer(n_cross, l)
        xp = _xattn_prompt(xp, g_x, _Layer(wq_all, l), mem_k, mem_v, _Layer(wo_all, l), l)
        xs = _xattn_sample(xs, g_x, _Layer(wq_perm, l), cache_k, cache_v, _Layer(wo_perm, l), l)
        g_f = _Layer(n_ffn, l)
        if l % 2 == 0:
            fw = tuple(_Layer(a, i) for a in fp)
            xp = _ffn(xp.reshape(B * L, D), g_f, *fw, tm=512).reshape(B, L, D)
            xs = _ffn(xs, g_f, *fw, tm=S)
        else:
            mw = _moe_layer_params(mp, i)
            gfin = g_fin if l == DEPTH - 1 else None
            xp = _moe_routed(xp.reshape(B * L, D), g_f, *mw, gfin=gfin).reshape(B, L, D)
            xs = _moe(xs, g_f, *mw, tm=S)
    y_prompt = xp
    y_sample = _final_norm(xs, g_fin, tm=S).reshape(S, 1, D)
    st = lambda d, k: jnp.stack(d[k])
    st_rows = lambda d, k: jnp.stack(d[k]).transpose(0, 2, 1, 3)
    return (y_prompt, y_sample,
            mem_k.reshape(DEPTH, B, MEM_LEN, X_HEADS, X_HEAD_DIM), mem_v.reshape(DEPTH, B, MEM_LEN, X_HEADS, X_HEAD_DIM),
            st(outs_p, 'ssm'), st(outs_p, 'ssm_conv'), st(outs_p, 'ret'), st(outs_p, 'pool'), st(outs_p, 'conv'),
            chain[0].swapaxes(3, 4), st_rows(outs_s, 'ssm_conv'), chain[1], st_rows(outs_s, 'pool'),
            st_rows(outs_s, 'conv'))
```

```python
import functools

import jax
import jax.numpy as jnp
import numpy as np
from jax import lax
from jax.experimental import pallas as pl
from jax.experimental.pallas import tpu as pltpu

F32 = jnp.float32
BF16 = jnp.bfloat16
HIGHEST = lax.Precision.HIGHEST

D_MODEL = 1024
DEPTH = 4
PAST_LEN = 16384
EPS = 1e-6
CHUNK = 128
LANES = 128

SSD_HEAD_DIM = 64
SSD_HEADS = 16
SSD_GROUPS = 2
SSD_STATE = 128
SSD_CONV = 4
SSD_CONV_CH = 1536
RET_HEADS = 8
RET_QK_DIM = 64
RET_V_DIM = 128
RET_QK_WIDTH = 512
RET_V_WIDTH = 1024
ROPE_BASE = 10000.0
EVEN_MIX = 2048

POOL_WIDTH = 512
POOL_WINDOWS = (2, 4, 8, 16)
POOL_GROUP_DIM = 128
POOL_STATE = 15
CONV_WIDTH = 512
CONV_K = 31
MEM_LEN = 256
X_HEADS = 4
X_HEAD_DIM = 256
D_FF = 2816
N_EXPERTS = 8
D_FF_EXPERT = 1792

E_Z, E_XBC, E_Q, E_K, E_V, E_G, E_DT, E_END = 0, 1024, 2560, 3072, 3584, 4608, 5632, 5760

VMEM_LIMIT = 56 * 1024 * 1024


def _cparams(*sem):
    return pltpu.CompilerParams(dimension_semantics=sem, vmem_limit_bytes=VMEM_LIMIT)


def _full(shape):
    n = len(shape)
    return pl.BlockSpec(shape, lambda *_: (0,) * n)


class _Layer:
    def __init__(self, arr, idx):
        self.arr, self.idx = arr, idx

    @property
    def spec(self):
        idx, rest = self.idx, (0,) * (self.arr.ndim - 1)
        return pl.BlockSpec((None,) + self.arr.shape[1:], lambda *_: (idx,) + rest, pipeline_mode=pl.Buffered(1))


def _specs(ops):
    return [o.spec if isinstance(o, _Layer) else _full(o.shape) for o in ops]


def _arrs(ops):
    return [o.arr if isinstance(o, _Layer) else o for o in ops]


def _rms(x, g):
    return x * lax.rsqrt(jnp.mean(x * x, axis=-1, keepdims=True) + EPS) * g


def _rms_only(x):
    return x * lax.rsqrt(jnp.mean(x * x, axis=-1, keepdims=True) + EPS)


def _silu(x):
    return x * jax.nn.sigmoid(x)


def _softplus(x):
    return jnp.maximum(x, 0.0) + jnp.log1p(jnp.exp(-jnp.abs(x)))


def _bdot(a, b):
    return jnp.dot(a.astype(BF16), b.astype(BF16), preferred_element_type=F32)


def _bdot_t(a, b):
    return lax.dot_general(a.astype(BF16), b.astype(BF16), (((1,), (1,)), ((), ())),
                           preferred_element_type=F32)


def _pair_cols(a, p, lo):
    return jnp.where(lo, a[:, 2 * p:2 * p + 1], a[:, 2 * p + 1:2 * p + 2])


def _rotary(x, cos_f, sin_s, lane):
    parts = []
    first_half = (lane % 64) < 32
    for c in range(x.shape[1] // LANES):
        xc = x[:, c * LANES:(c + 1) * LANES]
        parts.append(jnp.where(first_half, pltpu.roll(xc, 96, 1), pltpu.roll(xc, 32, 1)))
    return x * cos_f + jnp.concatenate(parts, axis=1) * sin_s


def _even_prompt_kernel(x_ref, g_ref, win_ref, cw_ref, cb_ref, dtb_ref, alog_ref, dsk_ref, gn_ref,
                        cos_ref, sin_ref, rd_ref, rea_ref, rwe_ref, rgt_ref, wout_ref,
                        xo_ref, nconv_ref, nssm_ref, nret_ref,
                        xbuf, sstate, rstate, *, T):
    t = pl.program_id(1)
    nt = pl.num_programs(1)

    @pl.when(t == 0)
    def _():
        xbuf[0:8, :] = jnp.zeros((8, SSD_CONV_CH), F32)
        sstate[...] = jnp.zeros_like(sstate)
        rstate[...] = jnp.zeros_like(rstate)

    x = x_ref[0]
    hb = _rms(x, g_ref[...]).astype(BF16)

    def proj(a, b):
        return jnp.dot(hb, win_ref[:, a:b], preferred_element_type=F32)

    xbuf[8:8 + T, :] = proj(E_XBC, E_Q)
    conv = cb_ref[...]
    for k in range(SSD_CONV):
        conv = conv + cw_ref[k:k + 1, :] * xbuf[5 + k:5 + k + T, :]
    tail = xbuf[T:T + 8, :]
    xbuf[0:8, :] = tail

    @pl.when(t == nt - 1)
    def _():
        nconv_ref[0] = tail[5:8, :]

    xc = _silu(conv)
    dt = _softplus(proj(E_DT, E_END) + dtb_ref[...])
    la = dt * (-jnp.exp(alog_ref[...]))

    lane = lax.broadcasted_iota(jnp.int32, (CHUNK, LANES), 1)
    row = lax.broadcasted_iota(jnp.int32, (CHUNK, LANES), 0)
    lo = lane < 64
    causal = row >= lane
    ltri = causal.astype(F32)

    q_all = _rotary(proj(E_Q, E_K), cos_ref[...], sin_ref[...], lane[:1])
    k_all = _rotary(proj(E_K, E_V), cos_ref[...], sin_ref[...], lane[:1]) * (RET_QK_DIM ** -0.5)
    v_all = proj(E_V, E_G)

    y_chunks, o_chunks = [], []
    for c in range(T // CHUNK):
        r0 = c * CHUNK
        rows = slice(r0, r0 + CHUNK)
        acum = jnp.dot(ltri, la[rows], precision=HIGHEST, preferred_element_type=F32)
        acum_t = acum.T
        tot = acum[CHUNK - 1:CHUNK, :]
        ea = jnp.exp(acum)
        wend = jnp.exp(tot - acum)
        etot = jnp.exp(tot)
        dt_c = dt[rows]
        y_pairs = []
        for g in range(SSD_GROUPS):
            b_g = xc[rows, 1024 + g * 128:1024 + (g + 1) * 128]
            c_g = xc[rows, 1280 + g * 128:1280 + (g + 1) * 128]
            gram = _bdot_t(c_g, b_g)
            y_inter = _bdot(c_g, sstate[g])
            xw, et = [], []
            for pp in range(4):
                p = g * 4 + pp
                xs_p = xc[rows, p * LANES:(p + 1) * LANES]
                xdt = xs_p * _pair_cols(dt_c, p, lo)
                s_pair = []
                for h in (2 * p, 2 * p + 1):
                    diff = acum[:, h:h + 1] - acum_t[h:h + 1, :]
                    dec = jnp.where(causal, jnp.exp(jnp.minimum(diff, 0.0)), 0.0)
                    s_pair.append((gram * dec).astype(BF16))
                rhs = jnp.concatenate([jnp.where(lo, xdt, 0.0), jnp.where(lo, 0.0, xdt)], axis=0)
                y_p = _bdot(jnp.concatenate(s_pair, axis=1), rhs)
                y_p = y_p + y_inter[:, pp * LANES:(pp + 1) * LANES] * _pair_cols(ea, p, lo)
                y_p = y_p + dsk_ref[:, p * LANES:(p + 1) * LANES] * xs_p
                y_pairs.append(y_p)
                xw.append(xdt * _pair_cols(wend, p, lo))
                et.append(_pair_cols(etot, p, lo[:1]))
            kv = _bdot(b_g.T, jnp.concatenate(xw, axis=1))
            sstate[g] = sstate[g] * jnp.concatenate(et, axis=1) + kv
        y_chunks.append(jnp.concatenate(y_pairs, axis=1))
        o_heads = []
        for p in range(RET_HEADS // 2):
            q_p = q_all[rows, p * LANES:(p + 1) * LANES]
            k_p = k_all[rows, p * LANES:(p + 1) * LANES]
            st_p = rstate[p * LANES:(p + 1) * LANES, :]
            ktw = k_p.T * rwe_ref[p]
            for hh in range(2):
                h = 2 * p + hh
                q_h = jnp.where(lo, q_p, 0.0) if hh == 0 else jnp.where(lo, 0.0, q_p)
                v_h = v_all[rows, h * LANES:(h + 1) * LANES]
                sc = (_bdot_t(q_h, k_p) * rd_ref[h]).astype(BF16)
                o_h = _bdot(sc, v_h) + _bdot(q_h, st_p) * rea_ref[h]
                o_heads.append(_rms_only(o_h))
                r = slice(h * 64, (h + 1) * 64)
                rstate[r, :] = rstate[r, :] * rgt_ref[r, :] + _bdot(ktw[hh * 64:(hh + 1) * 64, :], v_h)
        o_chunks.append(jnp.concatenate(o_heads, axis=1))

    y = jnp.concatenate(y_chunks, axis=0) if len(y_chunks) > 1 else y_chunks[0]
    o = jnp.concatenate(o_chunks, axis=0) if len(o_chunks) > 1 else o_chunks[0]
    y = y * _silu(proj(E_Z, E_XBC))
    y = jnp.concatenate([_rms_only(y[:, :512]), _rms_only(y[:, 512:])], axis=1) * gn_ref[...]
    o = o * _silu(proj(E_G, E_DT))
    mixed = jnp.concatenate([y, o], axis=1).astype(BF16)
    xo_ref[0] = x + jnp.dot(mixed, wout_ref[...], preferred_element_type=F32)

    @pl.when(t == nt - 1)
    def _():
        nssm_ref[0] = sstate[...]
        nret_ref[0] = rstate[...]


def _pad_lanes(v):
    return jnp.pad(v.astype(F32), [(0, 0)] * (v.ndim - 1) + [(0, LANES - v.shape[-1])])[..., None, :]


def _rows(v):
    return v.astype(F32)[:, None, :]


def _even_params(p):
    w = p['w_in_even']
    cols = [w[..., 0:1024], w[..., 1024:2560], w[..., 2576:3088], w[..., 3088:3600], w[..., 3600:4624],
            w[..., 4624:5648], w[..., 2560:2576], jnp.zeros(w.shape[:2] + (LANES - SSD_HEADS,), w.dtype)]
    return (jnp.concatenate(cols, axis=-1).astype(BF16), p['ssd_conv_w'], _rows(p['ssd_conv_b']),
            _pad_lanes(p['ssd_dt_bias']), _pad_lanes(p['ssd_a_log']),
            _rows(jnp.repeat(p['ssd_d'], SSD_HEAD_DIM, axis=1)), _rows(p['ssd_norm']), p['w_out_even'].astype(BF16))


def _even_layer_params(ep, i):
    return tuple(_Layer(a, i) for a in ep)


def _rope_tables(pos):
    half = RET_QK_DIM // 2
    f = np.float32
    inv = np.power(f(ROPE_BASE), -np.arange(half, dtype=f) / f(half)).astype(f)
    ang = (np.asarray(pos, f)[:, None] * inv[None, :]).astype(f)
    cos, sin = np.cos(ang).astype(f), np.sin(ang).astype(f)
    cos_f = np.tile(np.concatenate([cos, cos], axis=1), (1, RET_HEADS))
    sin_s = np.tile(np.concatenate([-sin, sin], axis=1), (1, RET_HEADS))
    return jnp.asarray(cos_f), jnp.asarray(sin_s)


def _ret_consts():
    f = np.float32
    lg = np.log1p(-np.exp2(f(-5.0) - np.arange(RET_HEADS, dtype=f))).astype(f)
    acum = np.cumsum(np.broadcast_to(lg[:, None], (RET_HEADS, CHUNK)), axis=1, dtype=f)
    idx = np.arange(CHUNK)
    diff = acum[:, :, None] - acum[:, None, :]
    rd = np.where((idx[:, None] >= idx[None, :])[None], np.exp(np.minimum(diff, f(0.0))), f(0.0)).astype(f)
    rea = np.broadcast_to(np.exp(acum)[:, :, None], (RET_HEADS, CHUNK, RET_V_DIM)).astype(f)
    total = acum[:, -1]
    we = np.exp(total[:, None] - acum).astype(f)
    rwe = np.broadcast_to(we[:, None, :], (RET_HEADS, RET_QK_DIM, CHUNK)).reshape(RET_HEADS // 2, 128, CHUNK)
    rgt = np.broadcast_to(np.exp(total).astype(f)[:, None, None], (RET_HEADS, RET_QK_DIM, RET_V_DIM))
    return (jnp.asarray(rd), jnp.asarray(rea), jnp.asarray(rwe), jnp.asarray(rgt.reshape(512, RET_V_DIM)),
            jnp.asarray(lg))


def _even_prompt(x, g, win, cw, cb, dtb, alog, dsk, gn, wout, cos_f, sin_s, rconsts, T=256):
    B, L, D = x.shape
    rd, rea, rwe, rgt, _ = rconsts
    grid = (B, L // T)
    tab = pl.BlockSpec((T, RET_QK_WIDTH), lambda b, t: (t, 0))
    in_specs = ([pl.BlockSpec((1, T, D), lambda b, t: (b, t, 0))] + _specs([g, win, cw, cb, dtb, alog, dsk, gn])
                + [tab, tab] + _specs([rd, rea, rwe, rgt, wout]))
    out_shape = [
        jax.ShapeDtypeStruct((B, L, D), F32),
        jax.ShapeDtypeStruct((B, SSD_CONV - 1, SSD_CONV_CH), F32),
        jax.ShapeDtypeStruct((B, SSD_GROUPS, SSD_STATE, 512), F32),
        jax.ShapeDtypeStruct((B, 512, RET_V_DIM), F32),
    ]
    out_specs = [
        pl.BlockSpec((1, T, D), lambda b, t: (b, t, 0)),
        pl.BlockSpec((1, SSD_CONV - 1, SSD_CONV_CH), lambda b, t: (b, 0, 0)),
        pl.BlockSpec((1, SSD_GROUPS, SSD_STATE, 512), lambda b, t: (b, 0, 0, 0)),
        pl.BlockSpec((1, 512, RET_V_DIM), lambda b, t: (b, 0, 0)),
    ]
    scratch = [pltpu.VMEM((T + 8, SSD_CONV_CH), F32), pltpu.VMEM((SSD_GROUPS, SSD_STATE, 512), F32),
               pltpu.VMEM((512, RET_V_DIM), F32)]
    xo, nconv, nssm, nret = pl.pallas_call(
        functools.partial(_even_prompt_kernel, T=T), grid=grid, in_specs=in_specs, out_specs=out_specs,
        out_shape=out_shape, scratch_shapes=scratch, compiler_params=_cparams("parallel", "arbitrary"),
        name="even_prompt",
    )(*_arrs([x, g, win, cw, cb, dtb, alog, dsk, gn, cos_f, sin_s, rd, rea, rwe, rgt, wout]))
    nssm = nssm.reshape(B, SSD_GROUPS, SSD_STATE, 8, SSD_HEAD_DIM).transpose(0, 1, 3, 2, 4)
    nssm = nssm.reshape(B, SSD_HEADS, SSD_STATE, SSD_HEAD_DIM)
    nret = nret.reshape(B, RET_HEADS, RET_QK_DIM, RET_V_DIM)
    return xo, nconv, nssm, nret


SB = 8


def _even_sample_kernel(x_ref, g_ref, win_ref, cw_ref, cb_ref, dtb_ref, alog_ref, dsk_ref, gn_ref,
                        cos_ref, sin_ref, gam_ref, wout_ref, cst_ref, sst_ref, rst_ref,
                        xo_ref, nconv_ref, nsst_ref, nrst_ref,
                        hb_s, xc_s, dt_s, e_s, q_s, k_s, v_s, y_s, o_s):
    i = pl.program_id(0)
    n = pl.num_programs(0)
    lane = lax.broadcasted_iota(jnp.int32, (1, LANES), 1)
    lo = lane < 64

    @pl.when(i == 0)
    def _():
        hb = _rms(x_ref[...], g_ref[...]).astype(BF16)
        hb_s[...] = hb

        def proj(a, b):
            return jnp.dot(hb, win_ref[:, a:b], preferred_element_type=F32)

        xbc = proj(E_XBC, E_Q)
        conv = cb_ref[...] + cw_ref[3:4, :] * xbc
        for k in range(SSD_CONV - 1):
            conv = conv + cw_ref[k:k + 1, :] * cst_ref[0, k]
        nconv_ref[0] = cst_ref[0, 1]
        nconv_ref[1] = cst_ref[0, 2]
        nconv_ref[2] = xbc
        xc_s[...] = _silu(conv)
        dt = _softplus(proj(E_DT, E_END) + dtb_ref[...])
        dt_s[...] = dt
        e_s[...] = jnp.exp(dt * (-jnp.exp(alog_ref[...])))
        q_s[...] = _rotary(proj(E_Q, E_K), cos_ref[...], sin_ref[...], lane)
        k_s[...] = _rotary(proj(E_K, E_V), cos_ref[...], sin_ref[...], lane) * (RET_QK_DIM ** -0.5)
        v_s[...] = proj(E_V, E_G)

    r = pl.ds(pl.multiple_of(i * SB, SB), SB)
    xc = xc_s[r, :]
    dt = dt_s[r, :]
    er = e_s[r, :]
    pad = jnp.zeros((LANES - 2 * SB, LANES), F32)
    lane_sq = lax.broadcasted_iota(jnp.int32, (LANES, LANES), 1)
    for g in range(SSD_GROUPS):
        b_g = xc[:, 1024 + g * 128:1024 + (g + 1) * 128]
        c_g = xc[:, 1280 + g * 128:1280 + (g + 1) * 128]
        cb_dot = jnp.sum(c_g * b_g, axis=-1, keepdims=True)
        for pp in range(4):
            c = g * 4 + pp
            xs_p = xc[:, c * LANES:(c + 1) * LANES]
            xdt = xs_p * _pair_cols(dt, c, lo)
            e_p = _pair_cols(er, c, lo)
            cols = jnp.concatenate([xdt, e_p, pad], axis=0).T
            ycols = jnp.zeros((LANES, LANES), F32)
            for j in range(SB):
                h2 = jnp.concatenate([sst_ref[0, j, 2 * c], sst_ref[0, j, 2 * c + 1]], axis=0)
                new = cols[:, SB + j:SB + j + 1] * h2 + cols[:, j:j + 1] * b_g[j:j + 1, :]
                nsst_ref[0, j, 2 * c] = new[:SSD_HEAD_DIM]
                nsst_ref[0, j, 2 * c + 1] = new[SSD_HEAD_DIM:]
                ycol = jnp.sum(h2 * c_g[j:j + 1, :], axis=-1, keepdims=True)
                ycols = jnp.where(lane_sq == j, ycol, ycols)
            y_p = cb_dot * xdt + e_p * ycols.T[0:SB, :] + dsk_ref[:, c * LANES:(c + 1) * LANES] * xs_p
            y_s[r, c * LANES:(c + 1) * LANES] = y_p
    q = q_s[r, :]
    k = k_s[r, :]
    v = v_s[r, :]
    for c in range(RET_HEADS // 2):
        q_p = q[:, c * LANES:(c + 1) * LANES]
        k_p = k[:, c * LANES:(c + 1) * LANES]
        prod = q_p * k_p
        cols = jnp.concatenate([q_p, k_p, pad], axis=0).T
        for hh in range(2):
            h = 2 * c + hh
            hr = slice(hh * RET_QK_DIM, (hh + 1) * RET_QK_DIM)
            keep = lo if hh == 0 else jnp.logical_not(lo)
            qk = jnp.sum(jnp.where(keep, prod, 0.0), axis=-1, keepdims=True)
            v_h = v[:, h * LANES:(h + 1) * LANES]
            gam = gam_ref[:, h:h + 1]
            rows = []
            for j in range(SB):
                hst = rst_ref[0, j, h]
                nrst_ref[0, j, h] = gam * hst + cols[hr, SB + j:SB + j + 1] * v_h[j:j + 1, :]
                rows.append(jnp.sum(cols[hr, j:j + 1] * hst, axis=0, keepdims=True))
            o_s[r, h * LANES:(h + 1) * LANES] = _rms_only(qk * v_h + gam * jnp.concatenate(rows, axis=0))

    @pl.when(i == n - 1)
    def _():
        hb = hb_s[...]
        y = y_s[...] * _silu(jnp.dot(hb, win_ref[:, E_Z:E_XBC], preferred_element_type=F32))
        y = jnp.concatenate([_rms_only(y[:, :512]), _rms_only(y[:, 512:])], axis=1) * gn_ref[...]
        o = o_s[...] * _silu(jnp.dot(hb, win_ref[:, E_G:E_DT], preferred_element_type=F32))
        mixed = jnp.concatenate([y, o], axis=1).astype(BF16)
        xo_ref[...] = x_ref[...] + jnp.dot(mixed, wout_ref[...], preferred_element_type=F32)


def _even_sample_kernel_chained(*refs):
    _even_sample_kernel(*refs[:16], *refs[18:])


def _even_sample(x, g, win, cw, cb, dtb, alog, dsk, gn, wout, cos_f, sin_s, gam, conv_all, ssm_all, ret_all, layer,
                 prev=None):
    S, D = x.shape
    cshape = (SSD_CONV - 1, S, SSD_CONV_CH)
    sblk = lambda a: pl.BlockSpec((1, SB) + a.shape[2:], lambda i: (layer, i, 0, 0, 0))
    ops = [x, g, win, cw, cb, dtb, alog, dsk, gn, cos_f, sin_s, gam, wout]
    in_specs = _specs(ops) + [pl.BlockSpec((1,) + cshape, lambda i: (layer, 0, 0, 0)), sblk(ssm_all), sblk(ret_all)]
    args = _arrs(ops) + [conv_all, ssm_all, ret_all]
    body, aliases = _even_sample_kernel, {}
    if prev is not None:
        body, aliases = _even_sample_kernel_chained, {len(args): 2, len(args) + 1: 3}
        in_specs = in_specs + [pl.BlockSpec(memory_space=pl.ANY)] * 2
        args = args + list(prev)
    out_shape = [jax.ShapeDtypeStruct(sh, F32) for sh in (x.shape, cshape, ssm_all.shape, ret_all.shape)]
    out_specs = [_full(x.shape), _full(cshape), sblk(ssm_all), sblk(ret_all)]
    scratch = [pltpu.VMEM((S, D), BF16), pltpu.VMEM((S, SSD_CONV_CH), F32), pltpu.VMEM((S, LANES), F32),
               pltpu.VMEM((S, LANES), F32), pltpu.VMEM((S, RET_QK_WIDTH), F32), pltpu.VMEM((S, RET_QK_WIDTH), F32),
               pltpu.VMEM((S, RET_V_WIDTH), F32), pltpu.VMEM((S, D), F32), pltpu.VMEM((S, RET_V_WIDTH), F32)]
    return pl.pallas_call(
        body, grid=(S // SB,), in_specs=in_specs, out_specs=out_specs, out_shape=out_shape,
        scratch_shapes=scratch, input_output_aliases=aliases, compiler_params=_cparams("arbitrary"),
        name="even_sample",
    )(*args)


def _layernorm_silu(c, g, b):
    mu = jnp.mean(c, axis=-1, keepdims=True)
    var = jnp.mean(jnp.square(c - mu), axis=-1, keepdims=True)
    return _silu((c - mu) * lax.rsqrt(var + EPS) * g + b)


def _pool_project(parts, pw_ref, ps_ref):
    proj = [_bdot(parts[gi], pw_ref[gi]) for gi in range(len(POOL_WINDOWS))]
    return jnp.concatenate(proj, axis=1) * ps_ref[...]


def _odd_prompt_kernel(x_ref, g_ref, win_ref, pw_ref, ps_ref, cw_ref, cb_ref, cng_ref, cnb_ref, wout_ref,
                       xo_ref, npool_ref, nconv_ref, ubuf, gbuf, sbuf, *, T):
    t = pl.program_id(1)
    nt = pl.num_programs(1)

    @pl.when(t == 0)
    def _():
        ubuf[0:16, :] = jnp.zeros((16, POOL_WIDTH), F32)
        gbuf[0:32, :] = jnp.zeros((32, CONV_WIDTH), F32)

    x = x_ref[0]
    hb = _rms(x, g_ref[...]).astype(BF16)
    u = jnp.dot(hb, win_ref[:, 0:512], preferred_element_type=F32)
    ga = jnp.dot(hb, win_ref[:, 512:1024], preferred_element_type=F32)
    gb = jnp.dot(hb, win_ref[:, 1024:1536], preferred_element_type=F32)
    ubuf[16:16 + T, :] = u
    pos1 = (lax.broadcasted_iota(jnp.int32, (T, 1), 0) + t * T + 1).astype(F32)
    parts = []
    for gi, w in enumerate(POOL_WINDOWS):
        sl = slice(gi * POOL_GROUP_DIM, (gi + 1) * POOL_GROUP_DIM)
        s = u[:, sl]
        for k in range(1, w):
            s = s + ubuf[16 - k:16 - k + T, sl]
        parts.append(s / jnp.minimum(pos1, float(w)) - u[:, sl])
    pooled = _pool_project(parts, pw_ref, ps_ref)
    utail = ubuf[T:T + 16, :]
    ubuf[0:16, :] = utail
    glu = ga * jax.nn.sigmoid(gb)
    gbuf[32:32 + T, :] = glu
    c = cb_ref[...] + cw_ref[CONV_K - 1:CONV_K, :] * glu
    for res in range(8):
        src_ref = gbuf
        if res:
            sbuf[...] = gbuf[res:res + T + 24, :]
            src_ref = sbuf
        for q in range(4):
            k = 8 * q + res - 2
            if 0 <= k < CONV_K - 1:
                c = c + cw_ref[k:k + 1, :] * src_ref[8 * q:8 * q + T, :]
    gtail = gbuf[T:T + 32, :]
    gbuf[0:32, :] = gtail
    c = _layernorm_silu(c, cng_ref[...], cnb_ref[...])
    mixed = jnp.concatenate([pooled, c], axis=1).astype(BF16)
    xo_ref[0] = x + jnp.dot(mixed, wout_ref[...], preferred_element_type=F32)

    @pl.when(t == nt - 1)
    def _():
        npool_ref[0] = utail[1:16, :]
        nconv_ref[0] = gtail[2:32, :]


def _odd_params(p):
    return (p['w_in_odd'].astype(BF16), p['pool_w'].astype(BF16), _rows(p['pool_scale']), p['conv_w'],
            _rows(p['conv_b']), _rows(p['conv_norm_g']), _rows(p['conv_norm_b']), p['w_out_odd'].astype(BF16))


def _odd_layer_params(op, i):
    return tuple(_Layer(a, i) for a in op)


def _odd_prompt(x, g, win, pw, ps, cw, cb, cng, cnb, wout, T=256):
    B, L, D = x.shape
    consts = (g, win, pw, ps, cw, cb, cng, cnb, wout)
    in_specs = [pl.BlockSpec((1, T, D), lambda b, t: (b, t, 0))] + _specs(consts)
    out_shape = [jax.ShapeDtypeStruct((B, L, D), F32), jax.ShapeDtypeStruct((B, POOL_STATE, POOL_WIDTH), F32),
                 jax.ShapeDtypeStruct((B, CONV_K - 1, CONV_WIDTH), F32)]
    out_specs = [pl.BlockSpec((1, T, D), lambda b, t: (b, t, 0)),
                 pl.BlockSpec((1, POOL_STATE, POOL_WIDTH), lambda b, t: (b, 0, 0)),
                 pl.BlockSpec((1, CONV_K - 1, CONV_WIDTH), lambda b, t: (b, 0, 0))]
    scratch = [pltpu.VMEM((T + 16, POOL_WIDTH), F32), pltpu.VMEM((T + 32, CONV_WIDTH), F32),
               pltpu.VMEM((T + 24, CONV_WIDTH), F32)]
    return pl.pallas_call(
        functools.partial(_odd_prompt_kernel, T=T), grid=(B, L // T), in_specs=in_specs, out_specs=out_specs,
        out_shape=out_shape, scratch_shapes=scratch, compiler_params=_cparams("parallel", "arbitrary"),
        name="odd_prompt",
    )(x, *_arrs(consts))


def _odd_sample_kernel(x_ref, g_ref, win_ref, pw_ref, ps_ref, cw_ref, cb_ref, cng_ref, cnb_ref, wout_ref,
                       pst_ref, cst_ref, xo_ref, npool_ref, nconv_ref):
    x = x_ref[...]
    hb = _rms(x, g_ref[...]).astype(BF16)
    u = jnp.dot(hb, win_ref[:, 0:512], preferred_element_type=F32)
    ga = jnp.dot(hb, win_ref[:, 512:1024], preferred_element_type=F32)
    gb = jnp.dot(hb, win_ref[:, 1024:1536], preferred_element_type=F32)
    parts = []
    for gi, w in enumerate(POOL_WINDOWS):
        lo_, hi_ = gi * POOL_GROUP_DIM, (gi + 1) * POOL_GROUP_DIM
        s = u[:, lo_:hi_]
        for k in range(1, w):
            s = s + pst_ref[0, POOL_STATE - k, :, lo_:hi_]
        parts.append(s / float(w) - u[:, lo_:hi_])
    pooled = _pool_project(parts, pw_ref, ps_ref)
    for j in range(POOL_STATE - 1):
        npool_ref[j] = pst_ref[0, j + 1]
    npool_ref[POOL_STATE - 1] = u
    glu = ga * jax.nn.sigmoid(gb)
    c = cb_ref[...] + cw_ref[CONV_K - 1:CONV_K, :] * glu
    for k in range(CONV_K - 1):
        c = c + cw_ref[k:k + 1, :] * cst_ref[0, k]
    for j in range(CONV_K - 2):
        nconv_ref[j] = cst_ref[0, j + 1]
    nconv_ref[CONV_K - 2] = glu
    c = _layernorm_silu(c, cng_ref[...], cnb_ref[...])
    mixed = jnp.concatenate([pooled, c], axis=1).astype(BF16)
    xo_ref[...] = x + jnp.dot(mixed, wout_ref[...], preferred_element_type=F32)


def _odd_sample(x, g, win, pw, ps, cw, cb, cng, cnb, wout, pool_all, conv_all, layer):
    S, D = x.shape
    pshape = (POOL_STATE, S, POOL_WIDTH)
    cshape = (CONV_K - 1, S, CONV_WIDTH)
    consts = (x, g, win, pw, ps, cw, cb, cng, cnb, wout)
    lspec = lambda sh: pl.BlockSpec((1,) + sh, lambda i: (layer, 0, 0, 0))
    outs = (x.shape, pshape, cshape)
    return pl.pallas_call(
        _odd_sample_kernel, grid=(1,), in_specs=_specs(consts) + [lspec(pshape), lspec(cshape)],
        out_specs=[_full(sh) for sh in outs], out_shape=[jax.ShapeDtypeStruct(sh, F32) for sh in outs],
        compiler_params=_cparams("arbitrary"), name="odd_sample",
    )(*_arrs(consts), pool_all, conv_all)


def _mem_kv_kernel(m_ref, g_ref, w_ref, k_ref, v_ref):
    hb = _rms(m_ref[...], g_ref[0]).astype(BF16)
    kv = jnp.dot(hb, w_ref[0], preferred_element_type=F32)
    k_ref[0] = kv[:, :D_MODEL]
    v_ref[0] = kv[:, D_MODEL:]


def _mem_kv(mem, norm_mem, wkv, tm=512):
    M, D = mem.shape
    out = jax.ShapeDtypeStruct((DEPTH, M, D), F32)
    return pl.pallas_call(
        _mem_kv_kernel, grid=(DEPTH, M // tm),
        in_specs=[pl.BlockSpec((tm, D), lambda l, m: (m, 0)), pl.BlockSpec((1, 1, D), lambda l, m: (l, 0, 0)),
                  pl.BlockSpec((1, D, 2 * D), lambda l, m: (l, 0, 0))],
        out_specs=[pl.BlockSpec((1, tm, D), lambda l, m: (l, m, 0))] * 2, out_shape=[out, out],
        compiler_params=_cparams("parallel", "parallel"), name="mem_kv",
    )(mem, norm_mem[:, None, :], wkv)


def _xattn_prompt_kernel(x_ref, g_ref, wq_ref, k_ref, v_ref, wo_ref, xo_ref):
    x = x_ref[0]
    hb = _rms(x, g_ref[...]).astype(BF16)
    q = jnp.dot(hb, wq_ref[...], preferred_element_type=F32) * (X_HEAD_DIM ** -0.5)
    outs = []
    for h in range(X_HEADS):
        sl = slice(h * X_HEAD_DIM, (h + 1) * X_HEAD_DIM)
        s = _bdot_t(q[:, sl], k_ref[0, :, sl])
        p = jnp.exp(s - jnp.max(s, axis=-1, keepdims=True))
        outs.append(_bdot(p, v_ref[0, :, sl]) / jnp.sum(p, axis=-1, keepdims=True))
    o = jnp.concatenate(outs, axis=1).astype(BF16)
    xo_ref[0] = x + jnp.dot(o, wo_ref[...], preferred_element_type=F32)


def _xattn_prompt(x, g, wq, mk, mv, wo, l, T=512):
    B, L, D = x.shape
    xspec = pl.BlockSpec((1, T, D), lambda b, t: (b, t, 0))
    mspec = pl.BlockSpec((None, 1, MEM_LEN, D), lambda b, t: (l, b, 0, 0))
    return pl.pallas_call(
        _xattn_prompt_kernel, grid=(B, L // T),
        in_specs=[xspec] + _specs([g, wq]) + [mspec, mspec] + _specs([wo]),
        out_specs=xspec, out_shape=jax.ShapeDtypeStruct(x.shape, F32),
        compiler_params=_cparams("parallel", "parallel"), name="xattn_prompt",
    )(*_arrs([x, g, wq, mk, mv, wo]))


XB = 8


def _xattn_sample_kernel(x_ref, g_ref, wq_ref, k_ref, v_ref, wo_ref, xo_ref, q_s, o_s):
    i = pl.program_id(0)
    n = pl.num_programs(0)

    @pl.when(i == 0)
    def _():
        hb = _rms(x_ref[...], g_ref[...]).astype(BF16)
        q_s[...] = jnp.dot(hb, wq_ref[...], preferred_element_type=F32) * (X_HEAD_DIM ** -0.5)

    for j in range(XB):
        r = pl.ds(i * XB + j, 1)
        qrow = q_s[r, :]
        q8 = jnp.concatenate([qrow[:, c * LANES:(c + 1) * LANES] for c in range(8)], axis=0)
        part = jnp.sum(k_ref[0, j] * q8[None], axis=-1, keepdims=True)
        s = part + pltpu.roll(part, 4, 1)
        p = jnp.exp(s - jnp.max(s, axis=0, keepdims=True))
        o8 = jnp.sum(p * v_ref[0, j], axis=0) / jnp.sum(p, axis=0)
        o_s[r, :] = jnp.concatenate([o8[c:c + 1, :] for c in range(8)], axis=1)

    @pl.when(i == n - 1)
    def _():
        xo_ref[...] = x_ref[...] + jnp.dot(o_s[...].astype(BF16), wo_ref[...], preferred_element_type=F32)


def _cache_tiles(c):
    d, s, m = c.shape[:3]
    return c.reshape(d, s, m, X_HEADS, 2, LANES).transpose(0, 1, 2, 4, 3, 5).reshape(d, s, m, 8, LANES)


def _xattn_perm_weights(wq, wo):
    n, d = wq.shape[:2]
    wq_p = wq.reshape(n, d, X_HEADS, 2, LANES).transpose(0, 1, 3, 2, 4).reshape(n, d, d)
    wo_p = wo.reshape(n, X_HEADS, 2, LANES, d).transpose(0, 2, 1, 3, 4).reshape(n, d, d)
    return wq_p, wo_p


def _xattn_sample(x, g, wq_p, ck, cv, wo_p, l):
    S, D = x.shape
    cspec = pl.BlockSpec((1, XB, MEM_LEN, 8, LANES), lambda i: (l, i, 0, 0, 0))
    return pl.pallas_call(
        _xattn_sample_kernel, grid=(S // XB,),
        in_specs=_specs([x, g, wq_p]) + [cspec, cspec] + _specs([wo_p]),
        out_specs=_full(x.shape), out_shape=jax.ShapeDtypeStruct(x.shape, F32),
        scratch_shapes=[pltpu.VMEM((S, D), F32), pltpu.VMEM((S, D), F32)],
        compiler_params=_cparams("arbitrary"), name="xattn_sample",
    )(*_arrs([x, g, wq_p, ck, cv, wo_p]))


def _ffn_kernel(x_ref, g_ref, w1_ref, w3_ref, w2_ref, xo_ref):
    x = x_ref[...]
    hb = _rms(x, g_ref[...]).astype(BF16)
    a = jnp.dot(hb, w1_ref[...], preferred_element_type=F32)
    b = jnp.dot(hb, w3_ref[...], preferred_element_type=F32)
    xo_ref[...] = x + jnp.dot((_silu(a) * b).astype(BF16), w2_ref[...], preferred_element_type=F32)


def _ffn(x, g, w1, w3, w2, tm):
    M, D = x.shape
    xspec = pl.BlockSpec((tm, D), lambda m: (m, 0))
    return pl.pallas_call(
        _ffn_kernel, grid=(M // tm,),
        in_specs=[xspec] + _specs([g, w1, w3, w2]),
        out_specs=xspec, out_shape=jax.ShapeDtypeStruct(x.shape, F32),
        compiler_params=_cparams("parallel"), name="ffn",
    )(*_arrs([x, g, w1, w3, w2]))


def _router_logits(h, rw_ref, rb_ref):
    h_hi = h.astype(BF16)
    h_lo = (h - h_hi.astype(F32)).astype(BF16)
    a = jnp.dot(h_hi, rw_ref[...], preferred_element_type=F32)
    b = jnp.dot(h_lo, rw_ref[:, :LANES], preferred_element_type=F32)
    return a[:, :LANES] + a[:, LANES:] + b + rb_ref[...]


def _moe_kernel(x_ref, g_ref, rw_ref, rb_ref, w1_ref, w3_ref, w2_ref, xo_ref, hb_s, comb_s, acc_s):
    e = pl.program_id(1)
    lane = lax.broadcasted_iota(jnp.int32, (1, LANES), 1)

    @pl.when(e == 0)
    def _():
        h = _rms(x_ref[...], g_ref[...])
        hb_s[...] = h.astype(BF16)
        logits = _router_logits(h, rw_ref, rb_ref)
        m1 = jnp.max(logits, axis=-1, keepdims=True)
        i1 = jnp.min(jnp.where(logits == m1, lane, LANES), axis=-1, keepdims=True)
        rest = jnp.where(lane == i1, -jnp.inf, logits)
        m2 = jnp.max(rest, axis=-1, keepdims=True)
        i2 = jnp.min(jnp.where(rest == m2, lane, LANES), axis=-1, keepdims=True)
        e2 = jnp.exp(m2 - m1)
        comb_s[...] = jnp.where(lane == i1, 1.0 / (1.0 + e2), 0.0) + jnp.where(lane == i2, e2 / (1.0 + e2), 0.0)
        acc_s[...] = jnp.zeros_like(acc_s)

    hb = hb_s[...]
    a = jnp.dot(hb, w1_ref[0], preferred_element_type=F32)
    b = jnp.dot(hb, w3_ref[0], preferred_element_type=F32)
    y = jnp.dot((_silu(a) * b).astype(BF16), w2_ref[0], preferred_element_type=F32)
    gate = jnp.sum(jnp.where(lane == e, comb_s[...], 0.0), axis=-1, keepdims=True)
    acc_s[...] += gate * y

    @pl.when(e == pl.num_programs(1) - 1)
    def _():
        xo_ref[...] = x_ref[...] + acc_s[...]


def _moe_params(p):
    rw = jnp.pad(p['router_w'].astype(F32), ((0, 0), (0, 0), (0, LANES - N_EXPERTS)))
    rw_hi = rw.astype(BF16)
    rw_lo = (rw - rw_hi.astype(F32)).astype(BF16)
    rb = jnp.pad(p['router_b'].astype(F32), ((0, 0), (0, LANES - N_EXPERTS)), constant_values=-1e30)[:, None, :]
    return (jnp.concatenate([rw_hi, rw_lo], axis=-1), rb,
            p['moe_w1'].astype(BF16), p['moe_w3'].astype(BF16), p['moe_w2'].astype(BF16))


def _moe_layer_params(mp, i):
    return tuple(_Layer(a, i) for a in mp)


def _expert_spec(w, pick):
    layer = w.idx
    return pl.BlockSpec((None, 1) + w.arr.shape[2:], lambda *a: (layer, pick(*a), 0, 0))


def _moe(x, g, rw, rb, w1, w3, w2, tm):
    M, D = x.shape
    xspec = pl.BlockSpec((tm, D), lambda m, e: (m, 0))
    pick = lambda m, e: e
    return pl.pallas_call(
        _moe_kernel, grid=(M // tm, N_EXPERTS),
        in_specs=[xspec] + _specs([g, rw, rb]) + [_expert_spec(w, pick) for w in (w1, w3, w2)],
        out_specs=xspec, out_shape=jax.ShapeDtypeStruct(x.shape, F32),
        scratch_shapes=[pltpu.VMEM((tm, D), BF16), pltpu.VMEM((tm, LANES), F32), pltpu.VMEM((tm, D), F32)],
        compiler_params=_cparams("parallel", "arbitrary"), name="moe",
    )(*_arrs([x, g, rw, rb, w1, w3, w2]))


MOE_TM = 512


def _moe_route_kernel(x_ref, g_ref, rw_ref, rb_ref, meta_ref, cnt_ref, base_s, *, tm):
    @pl.when(pl.program_id(0) == 0)
    def _():
        base_s[...] = jnp.zeros_like(base_s)

    lane = lax.broadcasted_iota(jnp.int32, (1, LANES), 1)
    h = _rms(x_ref[...], g_ref[...])
    logits = _router_logits(h, rw_ref, rb_ref)
    m1 = jnp.max(logits, axis=-1, keepdims=True)
    i1 = jnp.min(jnp.where(logits == m1, lane, LANES), axis=-1, keepdims=True)
    rest = jnp.where(lane == i1, -jnp.inf, logits)
    m2 = jnp.max(rest, axis=-1, keepdims=True)
    i2 = jnp.min(jnp.where(rest == m2, lane, LANES), axis=-1, keepdims=True)
    e2 = jnp.exp(m2 - m1)
    oh1 = lane == i1
    oh2 = lane == i2
    oh = jnp.logical_or(oh1, oh2).astype(F32)
    strict = (lax.broadcasted_iota(jnp.int32, (tm, tm), 0) > lax.broadcasted_iota(jnp.int32, (tm, tm), 1))
    pos = jnp.dot(strict.astype(BF16), oh.astype(BF16), preferred_element_type=F32) + base_s[...]
    r1 = jnp.sum(jnp.where(oh1, pos, 0.0), axis=-1, keepdims=True)
    r2 = jnp.sum(jnp.where(oh2, pos, 0.0), axis=-1, keepdims=True)
    base_s[...] += jnp.sum(oh, axis=0, keepdims=True)
    cols = (i1.astype(F32), i2.astype(F32), r1, r2, 1.0 / (1.0 + e2), e2 / (1.0 + e2))
    meta = jnp.zeros((tm, LANES), F32)
    for c, val in enumerate(cols):
        meta = jnp.where(lane == c, val, meta)
    meta_ref[...] = meta
    cnt_ref[...] = base_s[...]


def _moe_dispatch_kernel(dest_ref, x_ref, g_ref, xg_ref, hbuf, sem, *, tm, M):
    m = pl.program_id(0)
    slot = m % 2
    base = m * tm
    hrows = hbuf.at[slot]
    hrows[...] = _rms(x_ref[...], g_ref[...])

    def body(r, c):
        for k in range(2):
            d = dest_ref[k * M + base + r]
            pltpu.make_async_copy(hrows.at[pl.ds(r, 1)], xg_ref.at[pl.ds(d, 1)], sem.at[slot]).start(priority=k)
        return c

    lax.fori_loop(0, tm, body, 0, unroll=8)

    def drain(s):
        for k in range(2):
            pltpu.make_async_copy(hbuf.at[s], xg_ref.at[pl.ds(0, tm)], sem.at[s]).wait()

    @pl.when(m > 0)
    def _():
        drain(1 - slot)

    @pl.when(m == pl.num_programs(0) - 1)
    def _():
        drain(slot)


def _moe_group_kernel(te_ref, tb_ref, nv_ref, tot_ref, xg_ref, w1_ref, w3_ref, w2_ref, y_ref):
    s = pl.program_id(0)

    @pl.when(s < tot_ref[0])
    def _():
        rows = lax.broadcasted_iota(jnp.int32, (MOE_TM, 1), 0)
        xt = jnp.where(rows < nv_ref[s], xg_ref[...], 0.0).astype(BF16)
        a = jnp.dot(xt, w1_ref[0], preferred_element_type=F32)
        b = jnp.dot(xt, w3_ref[0], preferred_element_type=F32)
        y_ref[...] = jnp.dot((_silu(a) * b).astype(BF16), w2_ref[0], preferred_element_type=F32)


def _moe_combine_kernel(dest_ref, x_ref, meta_ref, gfin_ref, y_ref, o_ref, ybuf, sem, *, tm, M, final):
    m = pl.program_id(0)
    slot = m % 2

    def fetch(tile, s):
        base = tile * tm

        def body(r, c):
            for k in range(2):
                d = dest_ref[k * M + base + r]
                pltpu.make_async_copy(y_ref.at[pl.ds(d, 1)], ybuf.at[s, k, pl.ds(r, 1)], sem.at[s]).start(priority=k)
            return c

        lax.fori_loop(0, tm, body, 0, unroll=8)

    @pl.when(m == 0)
    def _():
        fetch(0, 0)

    @pl.when(m + 1 < pl.num_programs(0))
    def _():
        fetch(m + 1, 1 - slot)

    for k in range(2):
        pltpu.make_async_copy(y_ref.at[pl.ds(0, tm)], ybuf.at[slot, k], sem.at[slot]).wait()
    out = x_ref[...] + (meta_ref[:, 4:5] * ybuf[slot, 0] + meta_ref[:, 5:6] * ybuf[slot, 1])
    o_ref[...] = _rms(out, gfin_ref[...]) if final else out


def _moe_routed(x, g, rw, rb, w1, w3, w2, gfin=None, tm=512, tmd=256):
    M, D = x.shape
    TM = MOE_TM
    NT = 2 * M // TM + N_EXPERTS
    R = NT * TM
    xspec = pl.BlockSpec((tm, D), lambda m: (m, 0))
    meta, cnt = pl.pallas_call(
        functools.partial(_moe_route_kernel, tm=tm), grid=(M // tm,),
        in_specs=[xspec] + _specs([g, rw, rb]),
        out_specs=[pl.BlockSpec((tm, LANES), lambda m: (m, 0)), _full((1, LANES))],
        out_shape=[jax.ShapeDtypeStruct((M, LANES), F32), jax.ShapeDtypeStruct((1, LANES), F32)],
        scratch_shapes=[pltpu.VMEM((1, LANES), F32)], compiler_params=_cparams("arbitrary"), name="moe_route",
    )(*_arrs([x, g, rw, rb]))
    cnt = cnt[0, :N_EXPERTS].astype(jnp.int32)
    ntile = (cnt + TM - 1) // TM
    tile_end = jnp.cumsum(ntile)
    tile_start = tile_end - ntile
    total = tile_end[-1]
    s = jnp.arange(NT, dtype=jnp.int32)
    tb = jnp.maximum(jnp.minimum(s, total - 1), 0)
    te = jnp.minimum(jnp.sum((tb[:, None] >= tile_end[None, :]).astype(jnp.int32), axis=1), N_EXPERTS - 1)
    nv = jnp.clip(cnt[te] - (tb - tile_start[te]) * TM, 0, TM).astype(jnp.int32)
    row_off = (tile_start * TM).astype(jnp.int32)
    dest = jnp.concatenate([row_off[meta[:, 0].astype(jnp.int32)] + meta[:, 2].astype(jnp.int32),
                            row_off[meta[:, 1].astype(jnp.int32)] + meta[:, 3].astype(jnp.int32)])
    xdspec = pl.BlockSpec((tmd, D), lambda m, *_: (m, 0))
    xg = pl.pallas_call(
        functools.partial(_moe_dispatch_kernel, tm=tmd, M=M),
        grid_spec=pltpu.PrefetchScalarGridSpec(
            num_scalar_prefetch=1, grid=(M // tmd,),
            in_specs=[xdspec, g.spec],
            out_specs=pl.BlockSpec(memory_space=pl.ANY),
            scratch_shapes=[pltpu.VMEM((2, tmd, D), F32), pltpu.SemaphoreType.DMA((2,))]),
        out_shape=jax.ShapeDtypeStruct((R, D), F32), compiler_params=_cparams("arbitrary"), name="moe_dispatch",
    )(dest, x, g.arr)
    pick = lambda s, te, tb, nv, tot: te[s]
    tspec = pl.BlockSpec((TM, D), lambda s, te, tb, nv, tot: (tb[s], 0))
    y = pl.pallas_call(
        _moe_group_kernel,
        grid_spec=pltpu.PrefetchScalarGridSpec(
            num_scalar_prefetch=4, grid=(NT,),
            in_specs=[tspec] + [_expert_spec(w, pick) for w in (w1, w3, w2)],
            out_specs=tspec),
        out_shape=jax.ShapeDtypeStruct((R, D), F32), compiler_params=_cparams("arbitrary"), name="moe_group",
    )(te, tb, nv, total.reshape(1).astype(jnp.int32), xg, w1.arr, w3.arr, w2.arr)
    final = gfin is not None
    gfin = g if gfin is None else gfin
    return pl.pallas_call(
        functools.partial(_moe_combine_kernel, tm=tmd, M=M, final=final),
        grid_spec=pltpu.PrefetchScalarGridSpec(
            num_scalar_prefetch=1, grid=(M // tmd,),
            in_specs=[xdspec, pl.BlockSpec((tmd, LANES), lambda m, *_: (m, 0)), gfin.spec,
                      pl.BlockSpec(memory_space=pl.ANY)],
            out_specs=xdspec,
            scratch_shapes=[pltpu.VMEM((2, 2, tmd, D), F32), pltpu.SemaphoreType.DMA((2,))]),
        out_shape=jax.ShapeDtypeStruct((M, D), F32), compiler_params=_cparams("arbitrary"), name="moe_combine",
    )(dest, x, meta, gfin.arr, y)


def _final_norm_kernel(x_ref, g_ref, o_ref):
    o_ref[...] = _rms(x_ref[...], g_ref[...])


def _final_norm(x, g, tm):
    M, D = x.shape
    spec = pl.BlockSpec((tm, D), lambda m: (m, 0))
    return pl.pallas_call(
        _final_norm_kernel, grid=(M // tm,), in_specs=[spec] + _specs([g]), out_specs=spec,
        out_shape=jax.ShapeDtypeStruct(x.shape, F32), compiler_params=_cparams("parallel"), name="final_norm",
    )(*_arrs([x, g]))


def kernel(x_prompt, x_sample, state_ssm, state_ssm_conv, state_ret, state_pool, state_conv, cache_mem_k, cache_mem_v, mem_prompt, norm_mix, norm_cross, norm_ffn, norm_mem, norm_final, w_in_even, ssd_conv_w, ssd_conv_b, ssd_dt_bias, ssd_a_log, ssd_d, ssd_norm, w_out_even, ffn_w1, ffn_w3, ffn_w2, w_in_odd, pool_w, pool_scale, conv_w, conv_b, conv_norm_g, conv_norm_b, w_out_odd, router_w, router_b, moe_w1, moe_w3, moe_w2, xattn_q, xattn_k, xattn_v, xattn_o):
    p = dict(norm_mix=norm_mix, norm_cross=norm_cross, norm_ffn=norm_ffn, norm_mem=norm_mem,
             w_in_even=w_in_even, ssd_conv_w=ssd_conv_w, ssd_conv_b=ssd_conv_b, ssd_dt_bias=ssd_dt_bias,
             ssd_a_log=ssd_a_log, ssd_d=ssd_d, ssd_norm=ssd_norm, w_out_even=w_out_even,
             ffn_w1=ffn_w1, ffn_w3=ffn_w3, ffn_w2=ffn_w2,
             w_in_odd=w_in_odd, pool_w=pool_w, pool_scale=pool_scale, conv_w=conv_w, conv_b=conv_b,
             conv_norm_g=conv_norm_g, conv_norm_b=conv_norm_b, w_out_odd=w_out_odd,
             router_w=router_w, router_b=router_b, moe_w1=moe_w1, moe_w3=moe_w3, moe_w2=moe_w2,
             xattn_q=xattn_q, xattn_k=xattn_k, xattn_v=xattn_v, xattn_o=xattn_o)
    B, L, D = x_prompt.shape
    S = x_sample.shape[0]
    rconsts = _ret_consts()
    cos_p, sin_p = _rope_tables(np.arange(L, dtype=np.float32))
    cos_s, sin_s = _rope_tables(np.float32(PAST_LEN) + np.arange(1, dtype=np.float32))
    gam = _pad_lanes(jnp.exp(rconsts[4]))

    mem_k, mem_v = _mem_kv(mem_prompt.reshape(B * MEM_LEN, D), norm_mem,
                           jnp.concatenate([xattn_k, xattn_v], axis=2).astype(BF16))
    mem_k = mem_k.reshape(DEPTH, B, MEM_LEN, D)
    mem_v = mem_v.reshape(DEPTH, B, MEM_LEN, D)
    cache_k = _cache_tiles(cache_mem_k)
    cache_v = _cache_tiles(cache_mem_v)
    sconv_rows = state_ssm_conv.transpose(0, 2, 1, 3)
    ssm_rows = state_ssm.swapaxes(3, 4)
    chain = None
    pool_rows = state_pool.transpose(0, 2, 1, 3)
    conv_rows = state_conv.transpose(0, 2, 1, 3)

    xp = x_prompt
    xs = x_sample.reshape(S, D)
    outs_p = dict(ssm=[], ssm_conv=[], ret=[], pool=[], conv=[])
    outs_s = dict(ssm=[], ssm_conv=[], ret=[], pool=[], conv=[])
    ep, op, mp = _even_params(p), _odd_params(p), _moe_params(p)
    fp = (ffn_w1.astype(BF16), ffn_w3.astype(BF16), ffn_w2.astype(BF16))
    wq_all, wo_all = xattn_q.astype(BF16), xattn_o.astype(BF16)
    wq_perm, wo_perm = _xattn_perm_weights(wq_all, wo_all)
    n_mix, n_cross, n_ffn = _rows(norm_mix), _rows(norm_cross), _rows(norm_ffn)
    g_fin = _Layer(_rows(norm_final[None]), 0)
    for l in range(DEPTH):
        i = l // 2
        g_mix = _Layer(n_mix, l)
        if l % 2 == 0:
            ew = _even_layer_params(ep, i)
            xp, c_st, s_st, r_st = _even_prompt(xp, g_mix, *ew, cos_p, sin_p, rconsts)
            outs_p['ssm_conv'].append(c_st); outs_p['ssm'].append(s_st); outs_p['ret'].append(r_st)
            xs, c_st, *chain = _even_sample(xs, g_mix, *ew, cos_s, sin_s, gam,
                                            sconv_rows, ssm_rows, state_ret, i, chain)
            outs_s['ssm_conv'].append(c_st)
        else:
            ow = _odd_layer_params(op, i)
            xp, p_st, c_st = _odd_prompt(xp, g_mix, *ow)
            outs_p['pool'].append(p_st); outs_p['conv'].append(c_st)
            xs, p_st, c_st = _odd_sample(xs, g_mix, *ow, pool_rows, conv_rows, i)
            outs_s['pool'].append(p_st); outs_s['conv'].append(c_st)
        g_x = _Layer(n_cross, l)
        xp = _xattn_prompt(xp, g_x, _Layer(wq_all, l), mem_k, mem_v, _Layer(wo_all, l), l)
        xs = _xattn_sample(xs, g_x, _Layer(wq_perm, l), cache_k, cache_v, _Layer(wo_perm, l), l)
        g_f = _Layer(n_ffn, l)
        if l % 2 == 0:
            fw = tuple(_Layer(a, i) for a in fp)
            xp = _ffn(xp.reshape(B * L, D), g_f, *fw, tm=512).reshape(B, L, D)
            xs = _ffn(xs, g_f, *fw, tm=S)
        else:
            mw = _moe_layer_params(mp, i)
            gfin = g_fin if l == DEPTH - 1 else None
            xp = _moe_routed(xp.reshape(B * L, D), g_f, *mw, gfin=gfin).reshape(B, L, D)
            xs = _moe(xs, g_f, *mw, tm=S)
    y_prompt = xp
    y_sample = _final_norm(xs, g_fin, tm=S).reshape(S, 1, D)
    st = lambda d, k: jnp.stack(d[k])
    st_rows = lambda d, k: jnp.stack(d[k]).transpose(0, 2, 1, 3)
    return (y_prompt, y_sample,
            mem_k.reshape(DEPTH, B, MEM_LEN, X_HEADS, X_HEAD_DIM), mem_v.reshape(DEPTH, B, MEM_LEN, X_HEADS, X_HEAD_DIM),
            st(outs_p, 'ssm'), st(outs_p, 'ssm_conv'), st(outs_p, 'ret'), st(outs_p, 'pool'), st(outs_p, 'conv'),
            chain[0].swapaxes(3, 4), st_rows(outs_s, 'ssm_conv'), chain[1], st_rows(outs_s, 'pool'),
            st_rows(outs_s, 'conv'))
```

```python
import functools

import jax
import jax.numpy as jnp
import numpy as np
from jax import lax
from jax.experimental import pallas as pl
from jax.experimental.pallas import tpu as pltpu

F32 = jnp.float32
BF16 = jnp.bfloat16
HIGHEST = lax.Precision.HIGHEST

D_MODEL = 1024
DEPTH = 4
PAST_LEN = 16384
EPS = 1e-6
CHUNK = 128
LANES = 128

SSD_HEAD_DIM = 64
SSD_HEADS = 16
SSD_GROUPS = 2
SSD_STATE = 128
SSD_CONV = 4
SSD_CONV_CH = 1536
RET_HEADS = 8
RET_QK_DIM = 64
RET_V_DIM = 128
RET_QK_WIDTH = 512
RET_V_WIDTH = 1024
ROPE_BASE = 10000.0
EVEN_MIX = 2048

POOL_WIDTH = 512
POOL_WINDOWS = (2, 4, 8, 16)
POOL_GROUP_DIM = 128
POOL_STATE = 15
CONV_WIDTH = 512
CONV_K = 31
MEM_LEN = 256
X_HEADS = 4
X_HEAD_DIM = 256
D_FF = 2816
N_EXPERTS = 8
D_FF_EXPERT = 1792

E_Z, E_XBC, E_Q, E_K, E_V, E_G, E_DT, E_END = 0, 1024, 2560, 3072, 3584, 4608, 5632, 5760

VMEM_LIMIT = 56 * 1024 * 1024


def _cparams(*sem):
    return pltpu.CompilerParams(dimension_semantics=sem, vmem_limit_bytes=VMEM_LIMIT)


def _full(shape):
    n = len(shape)
    return pl.BlockSpec(shape, lambda *_: (0,) * n)


class _Layer:
    def __init__(self, arr, idx):
        self.arr, self.idx = arr, idx

    @property
    def spec(self):
        idx, rest = self.idx, (0,) * (self.arr.ndim - 1)
        return pl.BlockSpec((None,) + self.arr.shape[1:], lambda *_: (idx,) + rest, pipeline_mode=pl.Buffered(1))


def _specs(ops):
    return [o.spec if isinstance(o, _Layer) else _full(o.shape) for o in ops]


def _arrs(ops):
    return [o.arr if isinstance(o, _Layer) else o for o in ops]


def _rms(x, g):
    return x * lax.rsqrt(jnp.mean(x * x, axis=-1, keepdims=True) + EPS) * g


def _rms_only(x):
    return x * lax.rsqrt(jnp.mean(x * x, axis=-1, keepdims=True) + EPS)


def _silu(x):
    return x * jax.nn.sigmoid(x)


def _softplus(x):
    return jnp.maximum(x, 0.0) + jnp.log1p(jnp.exp(-jnp.abs(x)))


def _bdot(a, b):
    return jnp.dot(a.astype(BF16), b.astype(BF16), preferred_element_type=F32)


def _bdot_t(a, b):
    return lax.dot_general(a.astype(BF16), b.astype(BF16), (((1,), (1,)), ((), ())),
                           preferred_element_type=F32)


def _pair_cols(a, p, lo):
    return jnp.where(lo, a[:, 2 * p:2 * p + 1], a[:, 2 * p + 1:2 * p + 2])


def _rotary(x, cos_f, sin_s, lane):
    parts = []
    first_half = (lane % 64) < 32
    for c in range(x.shape[1] // LANES):
        xc = x[:, c * LANES:(c + 1) * LANES]
        parts.append(jnp.where(first_half, pltpu.roll(xc, 96, 1), pltpu.roll(xc, 32, 1)))
    return x * cos_f + jnp.concatenate(parts, axis=1) * sin_s


def _even_prompt_kernel(x_ref, g_ref, win_ref, cw_ref, cb_ref, dtb_ref, alog_ref, dsk_ref, gn_ref,
                        cos_ref, sin_ref, rd_ref, rea_ref, rwe_ref, rgt_ref, wout_ref,
                        xo_ref, nconv_ref, nssm_ref, nret_ref,
                        xbuf, sstate, rstate, *, T):
    t = pl.program_id(1)
    nt = pl.num_programs(1)

    @pl.when(t == 0)
    def _():
        xbuf[0:8, :] = jnp.zeros((8, SSD_CONV_CH), F32)
        sstate[...] = jnp.zeros_like(sstate)
        rstate[...] = jnp.zeros_like(rstate)

    x = x_ref[0]
    hb = _rms(x, g_ref[...]).astype(BF16)

    def proj(a, b):
        return jnp.dot(hb, win_ref[:, a:b], preferred_element_type=F32)

    xbuf[8:8 + T, :] = proj(E_XBC, E_Q)
    conv = cb_ref[...]
    for k in range(SSD_CONV):
        conv = conv + cw_ref[k:k + 1, :] * xbuf[5 + k:5 + k + T, :]
    tail = xbuf[T:T + 8, :]
    xbuf[0:8, :] = tail

    @pl.when(t == nt - 1)
    def _():
        nconv_ref[0] = tail[5:8, :]

    xc = _silu(conv)
    dt = _softplus(proj(E_DT, E_END) + dtb_ref[...])
    la = dt * (-jnp.exp(alog_ref[...]))

    lane = lax.broadcasted_iota(jnp.int32, (CHUNK, LANES), 1)
    row = lax.broadcasted_iota(jnp.int32, (CHUNK, LANES), 0)
    lo = lane < 64
    causal = row >= lane
    ltri = causal.astype(F32)

    q_all = _rotary(proj(E_Q, E_K), cos_ref[...], sin_ref[...], lane[:1])
    k_all = _rotary(proj(E_K, E_V), cos_ref[...], sin_ref[...], lane[:1]) * (RET_QK_DIM ** -0.5)
    v_all = proj(E_V, E_G)

    y_chunks, o_chunks = [], []
    for c in range(T // CHUNK):
        r0 = c * CHUNK
        rows = slice(r0, r0 + CHUNK)
        acum = jnp.dot(ltri, la[rows], precision=HIGHEST, preferred_element_type=F32)
        acum_t = acum.T
        tot = acum[CHUNK - 1:CHUNK, :]
        ea = jnp.exp(acum)
        wend = jnp.exp(tot - acum)
        etot = jnp.exp(tot)
        dt_c = dt[rows]
        y_pairs = []
        for g in range(SSD_GROUPS):
            b_g = xc[rows, 1024 + g * 128:1024 + (g + 1) * 128]
            c_g = xc[rows, 1280 + g * 128:1280 + (g + 1) * 128]
            gram = _bdot_t(c_g, b_g)
            y_inter = _bdot(c_g, sstate[g])
            xw, et = [], []
            for pp in range(4):
                p = g * 4 + pp
                xs_p = xc[rows, p * LANES:(p + 1) * LANES]
                xdt = xs_p * _pair_cols(dt_c, p, lo)
                s_pair = []
                for h in (2 * p, 2 * p + 1):
                    diff = acum[:, h:h + 1] - acum_t[h:h + 1, :]
                    dec = jnp.where(causal, jnp.exp(jnp.minimum(diff, 0.0)), 0.0)
                    s_pair.append((gram * dec).astype(BF16))
                rhs = jnp.concatenate([jnp.where(lo, xdt, 0.0), jnp.where(lo, 0.0, xdt)], axis=0)
                y_p = _bdot(jnp.concatenate(s_pair, axis=1), rhs)
                y_p = y_p + y_inter[:, pp * LANES:(pp + 1) * LANES] * _pair_cols(ea, p, lo)
                y_p = y_p + dsk_ref[:, p * LANES:(p + 1) * LANES] * xs_p
                y_pairs.append(y_p)
                xw.append(xdt * _pair_cols(wend, p, lo))
                et.append(_pair_cols(etot, p, lo[:1]))
            kv = _bdot(b_g.T, jnp.concatenate(xw, axis=1))
            sstate[g] = sstate[g] * jnp.concatenate(et, axis=1) + kv
        y_chunks.append(jnp.concatenate(y_pairs, axis=1))
        o_heads = []
        for p in range(RET_HEADS // 2):
            q_p = q_all[rows, p * LANES:(p + 1) * LANES]
            k_p = k_all[rows, p * LANES:(p + 1) * LANES]
            st_p = rstate[p * LANES:(p + 1) * LANES, :]
            ktw = k_p.T * rwe_ref[p]
            for hh in range(2):
                h = 2 * p + hh
                q_h = jnp.where(lo, q_p, 0.0) if hh == 0 else jnp.where(lo, 0.0, q_p)
                v_h = v_all[rows, h * LANES:(h + 1) * LANES]
                sc = (_bdot_t(q_h, k_p) * rd_ref[h]).astype(BF16)
                o_h = _bdot(sc, v_h) + _bdot(q_h, st_p) * rea_ref[h]
                o_heads.append(_rms_only(o_h))
                r = slice(h * 64, (h + 1) * 64)
                rstate[r, :] = rstate[r, :] * rgt_ref[r, :] + _bdot(ktw[hh * 64:(hh + 1) * 64, :], v_h)
        o_chunks.append(jnp.concatenate(o_heads, axis=1))

    y = jnp.concatenate(y_chunks, axis=0) if len(y_chunks) > 1 else y_chunks[0]
    o = jnp.concatenate(o_chunks, axis=0) if len(o_chunks) > 1 else o_chunks[0]
    y = y * _silu(proj(E_Z, E_XBC))
    y = jnp.concatenate([_rms_only(y[:, :512]), _rms_only(y[:, 512:])], axis=1) * gn_ref[...]
    o = o * _silu(proj(E_G, E_DT))
    mixed = jnp.concatenate([y, o], axis=1).astype(BF16)
    xo_ref[0] = x + jnp.dot(mixed, wout_ref[...], preferred_element_type=F32)

    @pl.when(t == nt - 1)
    def _():
        nssm_ref[0] = sstate[...]
        nret_ref[0] = rstate[...]


def _pad_lanes(v):
    return jnp.pad(v.astype(F32), [(0, 0)] * (v.ndim - 1) + [(0, LANES - v.shape[-1])])[..., None, :]


def _rows(v):
    return v.astype(F32)[:, None, :]


def _even_params(p):
    w = p['w_in_even']
    cols = [w[..., 0:1024], w[..., 1024:2560], w[..., 2576:3088], w[..., 3088:3600], w[..., 3600:4624],
            w[..., 4624:5648], w[..., 2560:2576], jnp.zeros(w.shape[:2] + (LANES - SSD_HEADS,), w.dtype)]
    return (jnp.concatenate(cols, axis=-1).astype(BF16), p['ssd_conv_w'], _rows(p['ssd_conv_b']),
            _pad_lanes(p['ssd_dt_bias']), _pad_lanes(p['ssd_a_log']),
            _rows(jnp.repeat(p['ssd_d'], SSD_HEAD_DIM, axis=1)), _rows(p['ssd_norm']), p['w_out_even'].astype(BF16))


def _even_layer_params(ep, i):
    return tuple(_Layer(a, i) for a in ep)


def _rope_tables(pos):
    half = RET_QK_DIM // 2
    f = np.float32
    inv = np.power(f(ROPE_BASE), -np.arange(half, dtype=f) / f(half)).astype(f)
    ang = (np.asarray(pos, f)[:, None] * inv[None, :]).astype(f)
    cos, sin = np.cos(ang).astype(f), np.sin(ang).astype(f)
    cos_f = np.tile(np.concatenate([cos, cos], axis=1), (1, RET_HEADS))
    sin_s = np.tile(np.concatenate([-sin, sin], axis=1), (1, RET_HEADS))
    return jnp.asarray(cos_f), jnp.asarray(sin_s)


def _ret_consts():
    f = np.float32
    lg = np.log1p(-np.exp2(f(-5.0) - np.arange(RET_HEADS, dtype=f))).astype(f)
    acum = np.cumsum(np.broadcast_to(lg[:, None], (RET_HEADS, CHUNK)), axis=1, dtype=f)
    idx = np.arange(CHUNK)
    diff = acum[:, :, None] - acum[:, None, :]
    rd = np.where((idx[:, None] >= idx[None, :])[None], np.exp(np.minimum(diff, f(0.0))), f(0.0)).astype(f)
    rea = np.broadcast_to(np.exp(acum)[:, :, None], (RET_HEADS, CHUNK, RET_V_DIM)).astype(f)
    total = acum[:, -1]
    we = np.exp(total[:, None] - acum).astype(f)
    rwe = np.broadcast_to(we[:, None, :], (RET_HEADS, RET_QK_DIM, CHUNK)).reshape(RET_HEADS // 2, 128, CHUNK)
    rgt = np.broadcast_to(np.exp(total).astype(f)[:, None, None], (RET_HEADS, RET_QK_DIM, RET_V_DIM))
    return (jnp.asarray(rd), jnp.asarray(rea), jnp.asarray(rwe), jnp.asarray(rgt.reshape(512, RET_V_DIM)),
            jnp.asarray(lg))


def _even_prompt(x, g, win, cw, cb, dtb, alog, dsk, gn, wout, cos_f, sin_s, rconsts, T=256):
    B, L, D = x.shape
    rd, rea, rwe, rgt, _ = rconsts
    grid = (B, L // T)
    tab = pl.BlockSpec((T, RET_QK_WIDTH), lambda b, t: (t, 0))
    in_specs = ([pl.BlockSpec((1, T, D), lambda b, t: (b, t, 0))] + _specs([g, win, cw, cb, dtb, alog, dsk, gn])
                + [tab, tab] + _specs([rd, rea, rwe, rgt, wout]))
    out_shape = [
        jax.ShapeDtypeStruct((B, L, D), F32),
        jax.ShapeDtypeStruct((B, SSD_CONV - 1, SSD_CONV_CH), F32),
        jax.ShapeDtypeStruct((B, SSD_GROUPS, SSD_STATE, 512), F32),
        jax.ShapeDtypeStruct((B, 512, RET_V_DIM), F32),
    ]
    out_specs = [
        pl.BlockSpec((1, T, D), lambda b, t: (b, t, 0)),
        pl.BlockSpec((1, SSD_CONV - 1, SSD_CONV_CH), lambda b, t: (b, 0, 0)),
        pl.BlockSpec((1, SSD_GROUPS, SSD_STATE, 512), lambda b, t: (b, 0, 0, 0)),
        pl.BlockSpec((1, 512, RET_V_DIM), lambda b, t: (b, 0, 0)),
    ]
    scratch = [pltpu.VMEM((T + 8, SSD_CONV_CH), F32), pltpu.VMEM((SSD_GROUPS, SSD_STATE, 512), F32),
               pltpu.VMEM((512, RET_V_DIM), F32)]
    xo, nconv, nssm, nret = pl.pallas_call(
        functools.partial(_even_prompt_kernel, T=T), grid=grid, in_specs=in_specs, out_specs=out_specs,
        out_shape=out_shape, scratch_shapes=scratch, compiler_params=_cparams("parallel", "arbitrary"),
        name="even_prompt",
    )(*_arrs([x, g, win, cw, cb, dtb, alog, dsk, gn, cos_f, sin_s, rd, rea, rwe, rgt, wout]))
    nssm = nssm.reshape(B, SSD_GROUPS, SSD_STATE, 8, SSD_HEAD_DIM).transpose(0, 1, 3, 2, 4)
    nssm = nssm.reshape(B, SSD_HEADS, SSD_STATE, SSD_HEAD_DIM)
    nret = nret.reshape(B, RET_HEADS, RET_QK_DIM, RET_V_DIM)
    return xo, nconv, nssm, nret


SB = 8


def _even_sample_kernel(x_ref, g_ref, win_ref, cw_ref, cb_ref, dtb_ref, alog_ref, dsk_ref, gn_ref,
                        cos_ref, sin_ref, gam_ref, wout_ref, cst_ref, sst_ref, rst_ref,
                        xo_ref, nconv_ref, nsst_ref, nrst_ref,
                        hb_s, xc_s, dt_s, e_s, q_s, k_s, v_s, y_s, o_s):
    i = pl.program_id(0)
    n = pl.num_programs(0)
    lane = lax.broadcasted_iota(jnp.int32, (1, LANES), 1)
    lo = lane < 64

    @pl.when(i == 0)
    def _():
        hb = _rms(x_ref[...], g_ref[...]).astype(BF16)
        hb_s[...] = hb

        def proj(a, b):
            return jnp.dot(hb, win_ref[:, a:b], preferred_element_type=F32)

        xbc = proj(E_XBC, E_Q)
        conv = cb_ref[...] + cw_ref[3:4, :] * xbc
        for k in range(SSD_CONV - 1):
            conv = conv + cw_ref[k:k + 1, :] * cst_ref[0, k]
        nconv_ref[0] = cst_ref[0, 1]
        nconv_ref[1] = cst_ref[0, 2]
        nconv_ref[2] = xbc
        xc_s[...] = _silu(conv)
        dt = _softplus(proj(E_DT, E_END) + dtb_ref[...])
        dt_s[...] = dt
        e_s[...] = jnp.exp(dt * (-jnp.exp(alog_ref[...])))
        q_s[...] = _rotary(proj(E_Q, E_K), cos_ref[...], sin_ref[...], lane)
        k_s[...] = _rotary(proj(E_K, E_V), cos_ref[...], sin_ref[...], lane) * (RET_QK_DIM ** -0.5)
        v_s[...] = proj(E_V, E_G)

    r = pl.ds(pl.multiple_of(i * SB, SB), SB)
    xc = xc_s[r, :]
    dt = dt_s[r, :]
    er = e_s[r, :]
    pad = jnp.zeros((LANES - 2 * SB, LANES), F32)
    lane_sq = lax.broadcasted_iota(jnp.int32, (LANES, LANES), 1)
    for g in range(SSD_GROUPS):
        b_g = xc[:, 1024 + g * 128:1024 + (g + 1) * 128]
        c_g = xc[:, 1280 + g * 128:1280 + (g + 1) * 128]
        cb_dot = jnp.sum(c_g * b_g, axis=-1, keepdims=True)
        for pp in range(4):
            c = g * 4 + pp
            xs_p = xc[:, c * LANES:(c + 1) * LANES]
            xdt = xs_p * _pair_cols(dt, c, lo)
            e_p = _pair_cols(er, c, lo)
            cols = jnp.concatenate([xdt, e_p, pad], axis=0).T
            ycols = jnp.zeros((LANES, LANES), F32)
            for j in range(SB):
                h2 = jnp.concatenate([sst_ref[0, j, 2 * c], sst_ref[0, j, 2 * c + 1]], axis=0)
                new = cols[:, SB + j:SB + j + 1] * h2 + cols[:, j:j + 1] * b_g[j:j + 1, :]
                nsst_ref[0, j, 2 * c] = new[:SSD_HEAD_DIM]
                nsst_ref[0, j, 2 * c + 1] = new[SSD_HEAD_DIM:]
                ycol = jnp.sum(h2 * c_g[j:j + 1, :], axis=-1, keepdims=True)
                ycols = jnp.where(lane_sq == j, ycol, ycols)
            y_p = cb_dot * xdt + e_p * ycols.T[0:SB, :] + dsk_ref[:, c * LANES:(c + 1) * LANES] * xs_p
            y_s[r, c * LANES:(c + 1) * LANES] = y_p
    q = q_s[r, :]
    k = k_s[r, :]
    v = v_s[r, :]
    for c in range(RET_HEADS // 2):
        q_p = q[:, c * LANES:(c + 1) * LANES]
        k_p = k[:, c * LANES:(c + 1) * LANES]
        prod = q_p * k_p
        cols = jnp.concatenate([q_p, k_p, pad], axis=0).T
        for hh in range(2):
            h = 2 * c + hh
            hr = slice(hh * RET_QK_DIM, (hh + 1) * RET_QK_DIM)
            keep = lo if hh == 0 else jnp.logical_not(lo)
            qk = jnp.sum(jnp.where(keep, prod, 0.0), axis=-1, keepdims=True)
            v_h = v[:, h * LANES:(h + 1) * LANES]
            gam = gam_ref[:, h:h + 1]
            rows = []
            for j in range(SB):
                hst = rst_ref[0, j, h]
                nrst_ref[0, j, h] = gam * hst + cols[hr, SB + j:SB + j + 1] * v_h[j:j + 1, :]
                rows.append(jnp.sum(cols[hr, j:j + 1] * hst, axis=0, keepdims=True))
            o_s[r, h * LANES:(h + 1) * LANES] = _rms_only(qk * v_h + gam * jnp.concatenate(rows, axis=0))

    @pl.when(i == n - 1)
    def _():
        hb = hb_s[...]
        y = y_s[...] * _silu(jnp.dot(hb, win_ref[:, E_Z:E_XBC], preferred_element_type=F32))
        y = jnp.concatenate([_rms_only(y[:, :512]), _rms_only(y[:, 512:])], axis=1) * gn_ref[...]
        o = o_s[...] * _silu(jnp.dot(hb, win_ref[:, E_G:E_DT], preferred_element_type=F32))
        mixed = jnp.concatenate([y, o], axis=1).astype(BF16)
        xo_ref[...] = x_ref[...] + jnp.dot(mixed, wout_ref[...], preferred_element_type=F32)


def _even_sample_kernel_chained(*refs):
    _even_sample_kernel(*refs[:16], *refs[18:])


def _even_sample(x, g, win, cw, cb, dtb, alog, dsk, gn, wout, cos_f, sin_s, gam, conv_all, ssm_all, ret_all, layer,
                 prev=None):
    S, D = x.shape
    cshape = (SSD_CONV - 1, S, SSD_CONV_CH)
    sblk = lambda a: pl.BlockSpec((1, SB) + a.shape[2:], lambda i: (layer, i, 0, 0, 0))
    ops = [x, g, win, cw, cb, dtb, alog, dsk, gn, cos_f, sin_s, gam, wout]
    in_specs = _specs(ops) + [pl.BlockSpec((1,) + cshape, lambda i: (layer, 0, 0, 0)), sblk(ssm_all), sblk(ret_all)]
    args = _arrs(ops) + [conv_all, ssm_all, ret_all]
    body, aliases = _even_sample_kernel, {}
    if prev is not None:
        body, aliases = _even_sample_kernel_chained, {len(args): 2, len(args) + 1: 3}
        in_specs = in_specs + [pl.BlockSpec(memory_space=pl.ANY)] * 2
        args = args + list(prev)
    out_shape = [jax.ShapeDtypeStruct(sh, F32) for sh in (x.shape, cshape, ssm_all.shape, ret_all.shape)]
    out_specs = [_full(x.shape), _full(cshape), sblk(ssm_all), sblk(ret_all)]
    scratch = [pltpu.VMEM((S, D), BF16), pltpu.VMEM((S, SSD_CONV_CH), F32), pltpu.VMEM((S, LANES), F32),
               pltpu.VMEM((S, LANES), F32), pltpu.VMEM((S, RET_QK_WIDTH), F32), pltpu.VMEM((S, RET_QK_WIDTH), F32),
               pltpu.VMEM((S, RET_V_WIDTH), F32), pltpu.VMEM((S, D), F32), pltpu.VMEM((S, RET_V_WIDTH), F32)]
    return pl.pallas_call(
        body, grid=(S // SB,), in_specs=in_specs, out_specs=out_specs, out_shape=out_shape,
        scratch_shapes=scratch, input_output_aliases=aliases, compiler_params=_cparams("arbitrary"),
        name="even_sample",
    )(*args)


def _layernorm_silu(c, g, b):
    mu = jnp.mean(c, axis=-1, keepdims=True)
    var = jnp.mean(jnp.square(c - mu), axis=-1, keepdims=True)
    return _silu((c - mu) * lax.rsqrt(var + EPS) * g + b)


def _pool_project(parts, pw_ref, ps_ref):
    proj = [_bdot(parts[gi], pw_ref[gi]) for gi in range(len(POOL_WINDOWS))]
    return jnp.concatenate(proj, axis=1) * ps_ref[...]


def _odd_prompt_kernel(x_ref, g_ref, win_ref, pw_ref, ps_ref, cw_ref, cb_ref, cng_ref, cnb_ref, wout_ref,
                       xo_ref, npool_ref, nconv_ref, ubuf, gbuf, sbuf, *, T):
    t = pl.program_id(1)
    nt = pl.num_programs(1)

    @pl.when(t == 0)
    def _():
        ubuf[0:16, :] = jnp.zeros((16, POOL_WIDTH), F32)
        gbuf[0:32, :] = jnp.zeros((32, CONV_WIDTH), F32)

    x = x_ref[0]
    hb = _rms(x, g_ref[...]).astype(BF16)
    u = jnp.dot(hb, win_ref[:, 0:512], preferred_element_type=F32)
    ga = jnp.dot(hb, win_ref[:, 512:1024], preferred_element_type=F32)
    gb = jnp.dot(hb, win_ref[:, 1024:1536], preferred_element_type=F32)
    ubuf[16:16 + T, :] = u
    pos1 = (lax.broadcasted_iota(jnp.int32, (T, 1), 0) + t * T + 1).astype(F32)
    parts = []
    for gi, w in enumerate(POOL_WINDOWS):
        sl = slice(gi * POOL_GROUP_DIM, (gi + 1) * POOL_GROUP_DIM)
        s = u[:, sl]
        for k in range(1, w):
            s = s + ubuf[16 - k:16 - k + T, sl]
        parts.append(s / jnp.minimum(pos1, float(w)) - u[:, sl])
    pooled = _pool_project(parts, pw_ref, ps_ref)
    utail = ubuf[T:T + 16, :]
    ubuf[0:16, :] = utail
    glu = ga * jax.nn.sigmoid(gb)
    gbuf[32:32 + T, :] = glu
    c = cb_ref[...] + cw_ref[CONV_K - 1:CONV_K, :] * glu
    for res in range(8):
        src_ref = gbuf
        if res:
            sbuf[...] = gbuf[res:res + T + 24, :]
            src_ref = sbuf
        for q in range(4):
            k = 8 * q + res - 2
            if 0 <= k < CONV_K - 1:
                c = c + cw_ref[k:k + 1, :] * src_ref[8 * q:8 * q + T, :]
    gtail = gbuf[T:T + 32, :]
    gbuf[0:32, :] = gtail
    c = _layernorm_silu(c, cng_ref[...], cnb_ref[...])
    mixed = jnp.concatenate([pooled, c], axis=1).astype(BF16)
    xo_ref[0] = x + jnp.dot(mixed, wout_ref[...], preferred_element_type=F32)

    @pl.when(t == nt - 1)
    def _():
        npool_ref[0] = utail[1:16, :]
        nconv_ref[0] = gtail[2:32, :]


def _odd_params(p):
    return (p['w_in_odd'].astype(BF16), p['pool_w'].astype(BF16), _rows(p['pool_scale']), p['conv_w'],
            _rows(p['conv_b']), _rows(p['conv_norm_g']), _rows(p['conv_norm_b']), p['w_out_odd'].astype(BF16))


def _odd_layer_params(op, i):
    return tuple(_Layer(a, i) for a in op)


def _odd_prompt(x, g, win, pw, ps, cw, cb, cng, cnb, wout, T=256):
    B, L, D = x.shape
    consts = (g, win, pw, ps, cw, cb, cng, cnb, wout)
    in_specs = [pl.BlockSpec((1, T, D), lambda b, t: (b, t, 0))] + _specs(consts)
    out_shape = [jax.ShapeDtypeStruct((B, L, D), F32), jax.ShapeDtypeStruct((B, POOL_STATE, POOL_WIDTH), F32),
                 jax.ShapeDtypeStruct((B, CONV_K - 1, CONV_WIDTH), F32)]
    out_specs = [pl.BlockSpec((1, T, D), lambda b, t: (b, t, 0)),
                 pl.BlockSpec((1, POOL_STATE, POOL_WIDTH), lambda b, t: (b, 0, 0)),
                 pl.BlockSpec((1, CONV_K - 1, CONV_WIDTH), lambda b, t: (b, 0, 0))]
    scratch = [pltpu.VMEM((T + 16, POOL_WIDTH), F32), pltpu.VMEM((T + 32, CONV_WIDTH), F32),
               pltpu.VMEM((T + 24, CONV_WIDTH), F32)]
    return pl.pallas_call(
        functools.partial(_odd_prompt_kernel, T=T), grid=(B, L // T), in_specs=in_specs, out_specs=out_specs,
        out_shape=out_shape, scratch_shapes=scratch, compiler_params=_cparams("parallel", "arbitrary"),
        name="odd_prompt",
    )(x, *_arrs(consts))


def _odd_sample_kernel(x_ref, g_ref, win_ref, pw_ref, ps_ref, cw_ref, cb_ref, cng_ref, cnb_ref, wout_ref,
                       pst_ref, cst_ref, xo_ref, npool_ref, nconv_ref):
    x = x_ref[...]
    hb = _rms(x, g_ref[...]).astype(BF16)
    u = jnp.dot(hb, win_ref[:, 0:512], preferred_element_type=F32)
    ga = jnp.dot(hb, win_ref[:, 512:1024], preferred_element_type=F32)
    gb = jnp.dot(hb, win_ref[:, 1024:1536], preferred_element_type=F32)
    parts = []
    for gi, w in enumerate(POOL_WINDOWS):
        lo_, hi_ = gi * POOL_GROUP_DIM, (gi + 1) * POOL_GROUP_DIM
        s = u[:, lo_:hi_]
        for k in range(1, w):
            s = s + pst_ref[0, POOL_STATE - k, :, lo_:hi_]
        parts.append(s / float(w) - u[:, lo_:hi_])
    pooled = _pool_project(parts, pw_ref, ps_ref)
    for j in range(POOL_STATE - 1):
        npool_ref[j] = pst_ref[0, j + 1]
    npool_ref[POOL_STATE - 1] = u
    glu = ga * jax.nn.sigmoid(gb)
    c = cb_ref[...] + cw_ref[CONV_K - 1:CONV_K, :] * glu
    for k in range(CONV_K - 1):
        c = c + cw_ref[k:k + 1, :] * cst_ref[0, k]
    for j in range(CONV_K - 2):
        nconv_ref[j] = cst_ref[0, j + 1]
    nconv_ref[CONV_K - 2] = glu
    c = _layernorm_silu(c, cng_ref[...], cnb_ref[...])
    mixed = jnp.concatenate([pooled, c], axis=1).astype(BF16)
    xo_ref[...] = x + jnp.dot(mixed, wout_ref[...], preferred_element_type=F32)


def _odd_sample(x, g, win, pw, ps, cw, cb, cng, cnb, wout, pool_all, conv_all, layer):
    S, D = x.shape
    pshape = (POOL_STATE, S, POOL_WIDTH)
    cshape = (CONV_K - 1, S, CONV_WIDTH)
    consts = (x, g, win, pw, ps, cw, cb, cng, cnb, wout)
    lspec = lambda sh: pl.BlockSpec((1,) + sh, lambda i: (layer, 0, 0, 0))
    outs = (x.shape, pshape, cshape)
    return pl.pallas_call(
        _odd_sample_kernel, grid=(1,), in_specs=_specs(consts) + [lspec(pshape), lspec(cshape)],
        out_specs=[_full(sh) for sh in outs], out_shape=[jax.ShapeDtypeStruct(sh, F32) for sh in outs],
        compiler_params=_cparams("arbitrary"), name="odd_sample",
    )(*_arrs(consts), pool_all, conv_all)


def _mem_kv_kernel(m_ref, g_ref, w_ref, k_ref, v_ref):
    hb = _rms(m_ref[...], g_ref[0]).astype(BF16)
    kv = jnp.dot(hb, w_ref[0], preferred_element_type=F32)
    k_ref[0] = kv[:, :D_MODEL]
    v_ref[0] = kv[:, D_MODEL:]


def _mem_kv(mem, norm_mem, wkv, tm=512):
    M, D = mem.shape
    out = jax.ShapeDtypeStruct((DEPTH, M, D), F32)
    return pl.pallas_call(
        _mem_kv_kernel, grid=(DEPTH, M // tm),
        in_specs=[pl.BlockSpec((tm, D), lambda l, m: (m, 0)), pl.BlockSpec((1, 1, D), lambda l, m: (l, 0, 0)),
                  pl.BlockSpec((1, D, 2 * D), lambda l, m: (l, 0, 0))],
        out_specs=[pl.BlockSpec((1, tm, D), lambda l, m: (l, m, 0))] * 2, out_shape=[out, out],
        compiler_params=_cparams("parallel", "parallel"), name="mem_kv",
    )(mem, norm_mem[:, None, :], wkv)


def _xattn_prompt_kernel(x_ref, g_ref, wq_ref, k_ref, v_ref, wo_ref, xo_ref):
    x = x_ref[0]
    hb = _rms(x, g_ref[...]).astype(BF16)
    q = jnp.dot(hb, wq_ref[...], preferred_element_type=F32) * (X_HEAD_DIM ** -0.5)
    outs = []
    for h in range(X_HEADS):
        sl = slice(h * X_HEAD_DIM, (h + 1) * X_HEAD_DIM)
        s = _bdot_t(q[:, sl], k_ref[0, :, sl])
        p = jnp.exp(s - jnp.max(s, axis=-1, keepdims=True))
        outs.append(_bdot(p, v_ref[0, :, sl]) / jnp.sum(p, axis=-1, keepdims=True))
    o = jnp.concatenate(outs, axis=1).astype(BF16)
    xo_ref[0] = x + jnp.dot(o, wo_ref[...], preferred_element_type=F32)


def _xattn_prompt(x, g, wq, mk, mv, wo, l, T=512):
    B, L, D = x.shape
    xspec = pl.BlockSpec((1, T, D), lambda b, t: (b, t, 0))
    mspec = pl.BlockSpec((None, 1, MEM_LEN, D), lambda b, t: (l, b, 0, 0))
    return pl.pallas_call(
        _xattn_prompt_kernel, grid=(B, L // T),
        in_specs=[xspec] + _specs([g, wq]) + [mspec, mspec] + _specs([wo]),
        out_specs=xspec, out_shape=jax.ShapeDtypeStruct(x.shape, F32),
        compiler_params=_cparams("parallel", "parallel"), name="xattn_prompt",
    )(*_arrs([x, g, wq, mk, mv, wo]))


XB = 8


def _xattn_sample_kernel(x_ref, g_ref, wq_ref, k_ref, v_ref, wo_ref, xo_ref, q_s, o_s):
    i = pl.program_id(0)
    n = pl.num_programs(0)

    @pl.when(i == 0)
    def _():
        hb = _rms(x_ref[...], g_ref[...]).astype(BF16)
        q_s[...] = jnp.dot(hb, wq_ref[...], preferred_element_type=F32) * (X_HEAD_DIM ** -0.5)

    for j in range(XB):
        r = pl.ds(i * XB + j, 1)
        qrow = q_s[r, :]
        q8 = jnp.concatenate([qrow[:, c * LANES:(c + 1) * LANES] for c in range(8)], axis=0)
        part = jnp.sum(k_ref[0, j] * q8[None], axis=-1, keepdims=True)
        s = part + pltpu.roll(part, 4, 1)
        p = jnp.exp(s - jnp.max(s, axis=0, keepdims=True))
        o8 = jnp.sum(p * v_ref[0, j], axis=0) / jnp.sum(p, axis=0)
        o_s[r, :] = jnp.concatenate([o8[c:c + 1, :] for c in range(8)], axis=1)

    @pl.when(i == n - 1)
    def _():
        xo_ref[...] = x_ref[...] + jnp.dot(o_s[...].astype(BF16), wo_ref[...], preferred_element_type=F32)


def _cache_tiles(c):
    d, s, m = c.shape[:3]
    return c.reshape(d, s, m, X_HEADS, 2, LANES).transpose(0, 1, 2, 4, 3, 5).reshape(d, s, m, 8, LANES)


def _xattn_perm_weights(wq, wo):
    n, d = wq.shape[:2]
    wq_p = wq.reshape(n, d, X_HEADS, 2, LANES).transpose(0, 1, 3, 2, 4).reshape(n, d, d)
    wo_p = wo.reshape(n, X_HEADS, 2, LANES, d).transpose(0, 2, 1, 3, 4).reshape(n, d, d)
    return wq_p, wo_p


def _xattn_sample(x, g, wq_p, ck, cv, wo_p, l):
    S, D = x.shape
    cspec = pl.BlockSpec((1, XB, MEM_LEN, 8, LANES), lambda i: (l, i, 0, 0, 0))
    return pl.pallas_call(
        _xattn_sample_kernel, grid=(S // XB,),
        in_specs=_specs([x, g, wq_p]) + [cspec, cspec] + _specs([wo_p]),
        out_specs=_full(x.shape), out_shape=jax.ShapeDtypeStruct(x.shape, F32),
        scratch_shapes=[pltpu.VMEM((S, D), F32), pltpu.VMEM((S, D), F32)],
        compiler_params=_cparams("arbitrary"), name="xattn_sample",
    )(*_arrs([x, g, wq_p, ck, cv, wo_p]))


def _ffn_kernel(x_ref, g_ref, w1_ref, w3_ref, w2_ref, xo_ref):
    x = x_ref[...]
    hb = _rms(x, g_ref[...]).astype(BF16)
    a = jnp.dot(hb, w1_ref[...], preferred_element_type=F32)
    b = jnp.dot(hb, w3_ref[...], preferred_element_type=F32)
    xo_ref[...] = x + jnp.dot((_silu(a) * b).astype(BF16), w2_ref[...], preferred_element_type=F32)


def _ffn(x, g, w1, w3, w2, tm):
    M, D = x.shape
    xspec = pl.BlockSpec((tm, D), lambda m: (m, 0))
    return pl.pallas_call(
        _ffn_kernel, grid=(M // tm,),
        in_specs=[xspec] + _specs([g, w1, w3, w2]),
        out_specs=xspec, out_shape=jax.ShapeDtypeStruct(x.shape, F32),
        compiler_params=_cparams("parallel"), name="ffn",
    )(*_arrs([x, g, w1, w3, w2]))


def _router_logits(h, rw_ref, rb_ref):
    h_hi = h.astype(BF16)
    h_lo = (h - h_hi.astype(F32)).astype(BF16)
    a = jnp.dot(h_hi, rw_ref[...], preferred_element_type=F32)
    b = jnp.dot(h_lo, rw_ref[:, :LANES], preferred_element_type=F32)
    return a[:, :LANES] + a[:, LANES:] + b + rb_ref[...]


def _moe_kernel(x_ref, g_ref, rw_ref, rb_ref, w1_ref, w3_ref, w2_ref, xo_ref, hb_s, comb_s, acc_s):
    e = pl.program_id(1)
    lane = lax.broadcasted_iota(jnp.int32, (1, LANES), 1)

    @pl.when(e == 0)
    def _():
        h = _rms(x_ref[...], g_ref[...])
        hb_s[...] = h.astype(BF16)
        logits = _router_logits(h, rw_ref, rb_ref)
        m1 = jnp.max(logits, axis=-1, keepdims=True)
        i1 = jnp.min(jnp.where(logits == m1, lane, LANES), axis=-1, keepdims=True)
        rest = jnp.where(lane == i1, -jnp.inf, logits)
        m2 = jnp.max(rest, axis=-1, keepdims=True)
        i2 = jnp.min(jnp.where(rest == m2, lane, LANES), axis=-1, keepdims=True)
        e2 = jnp.exp(m2 - m1)
        comb_s[...] = jnp.where(lane == i1, 1.0 / (1.0 + e2), 0.0) + jnp.where(lane == i2, e2 / (1.0 + e2), 0.0)
        acc_s[...] = jnp.zeros_like(acc_s)

    hb = hb_s[...]
    a = jnp.dot(hb, w1_ref[0], preferred_element_type=F32)
    b = jnp.dot(hb, w3_ref[0], preferred_element_type=F32)
    y = jnp.dot((_silu(a) * b).astype(BF16), w2_ref[0], preferred_element_type=F32)
    gate = jnp.sum(jnp.where(lane == e, comb_s[...], 0.0), axis=-1, keepdims=True)
    acc_s[...] += gate * y

    @pl.when(e == pl.num_programs(1) - 1)
    def _():
        xo_ref[...] = x_ref[...] + acc_s[...]


def _moe_params(p):
    rw = jnp.pad(p['router_w'].astype(F32), ((0, 0), (0, 0), (0, LANES - N_EXPERTS)))
    rw_hi = rw.astype(BF16)
    rw_lo = (rw - rw_hi.astype(F32)).astype(BF16)
    rb = jnp.pad(p['router_b'].astype(F32), ((0, 0), (0, LANES - N_EXPERTS)), constant_values=-1e30)[:, None, :]
    return (jnp.concatenate([rw_hi, rw_lo], axis=-1), rb,
            p['moe_w1'].astype(BF16), p['moe_w3'].astype(BF16), p['moe_w2'].astype(BF16))


def _moe_layer_params(mp, i):
    return tuple(_Layer(a, i) for a in mp)


def _expert_spec(w, pick):
    layer = w.idx
    return pl.BlockSpec((None, 1) + w.arr.shape[2:], lambda *a: (layer, pick(*a), 0, 0))


def _moe(x, g, rw, rb, w1, w3, w2, tm):
    M, D = x.shape
    xspec = pl.BlockSpec((tm, D), lambda m, e: (m, 0))
    pick = lambda m, e: e
    return pl.pallas_call(
        _moe_kernel, grid=(M // tm, N_EXPERTS),
        in_specs=[xspec] + _specs([g, rw, rb]) + [_expert_spec(w, pick) for w in (w1, w3, w2)],
        out_specs=xspec, out_shape=jax.ShapeDtypeStruct(x.shape, F32),
        scratch_shapes=[pltpu.VMEM((tm, D), BF16), pltpu.VMEM((tm, LANES), F32), pltpu.VMEM((tm, D), F32)],
        compiler_params=_cparams("parallel", "arbitrary"), name="moe",
    )(*_arrs([x, g, rw, rb, w1, w3, w2]))


MOE_TM = 512


def _moe_route_kernel(x_ref, g_ref, rw_ref, rb_ref, meta_ref, mt_ref, cnt_ref, base_s, *, tm):
    @pl.when(pl.program_id(0) == 0)
    def _():
        base_s[...] = jnp.zeros_like(base_s)

    lane = lax.broadcasted_iota(jnp.int32, (1, LANES), 1)
    h = _rms(x_ref[...], g_ref[...])
    logits = _router_logits(h, rw_ref, rb_ref)
    m1 = jnp.max(logits, axis=-1, keepdims=True)
    i1 = jnp.min(jnp.where(logits == m1, lane, LANES), axis=-1, keepdims=True)
    rest = jnp.where(lane == i1, -jnp.inf, logits)
    m2 = jnp.max(rest, axis=-1, keepdims=True)
    i2 = jnp.min(jnp.where(rest == m2, lane, LANES), axis=-1, keepdims=True)
    e2 = jnp.exp(m2 - m1)
    oh1 = lane == i1
    oh2 = lane == i2
    oh = jnp.logical_or(oh1, oh2).astype(F32)
    strict = (lax.broadcasted_iota(jnp.int32, (tm, tm), 0) > lax.broadcasted_iota(jnp.int32, (tm, tm), 1))
    pos = jnp.dot(strict.astype(BF16), oh.astype(BF16), preferred_element_type=F32) + base_s[...]
    r1 = jnp.sum(jnp.where(oh1, pos, 0.0), axis=-1, keepdims=True)
    r2 = jnp.sum(jnp.where(oh2, pos, 0.0), axis=-1, keepdims=True)
    base_s[...] += jnp.sum(oh, axis=0, keepdims=True)
    cols = (i1.astype(F32), i2.astype(F32), r1, r2, 1.0 / (1.0 + e2), e2 / (1.0 + e2))
    meta = jnp.zeros((tm, LANES), F32)
    for c, val in enumerate(cols):
        meta = jnp.where(lane == c, val, meta)
    meta_ref[...] = meta
    meta_t = jnp.concatenate([meta[r * LANES:(r + 1) * LANES, :].T for r in range(tm // LANES)], axis=1)
    mt_ref[...] = meta_t[0:8, :]
    cnt_ref[...] = base_s[...]


def _moe_dispatch_kernel(dest_ref, x_ref, g_ref, xg_ref, hbuf, sem, *, tm, M):
    m = pl.program_id(0)
    slot = m % 2
    base = m * tm
    hrows = hbuf.at[slot]
    hrows[...] = _rms(x_ref[...], g_ref[...])

    def body(r, c):
        for k in range(2):
            d = dest_ref[k * M + base + r]
            pltpu.make_async_copy(hrows.at[pl.ds(r, 1)], xg_ref.at[pl.ds(d, 1)], sem.at[slot]).start(priority=k)
        return c

    lax.fori_loop(0, tm, body, 0, unroll=8)

    def drain(s):
        for k in range(2):
            pltpu.make_async_copy(hbuf.at[s], xg_ref.at[pl.ds(0, tm)], sem.at[s]).wait()

    @pl.when(m > 0)
    def _():
        drain(1 - slot)

    @pl.when(m == pl.num_programs(0) - 1)
    def _():
        drain(slot)


def _moe_group_kernel(te_ref, tb_ref, nv_ref, tot_ref, xg_ref, w1_ref, w3_ref, w2_ref, y_ref):
    s = pl.program_id(0)

    @pl.when(s < tot_ref[0])
    def _():
        rows = lax.broadcasted_iota(jnp.int32, (MOE_TM, 1), 0)
        xt = jnp.where(rows < nv_ref[s], xg_ref[...], 0.0).astype(BF16)
        a = jnp.dot(xt, w1_ref[0], preferred_element_type=F32)
        b = jnp.dot(xt, w3_ref[0], preferred_element_type=F32)
        y_ref[...] = jnp.dot((_silu(a) * b).astype(BF16), w2_ref[0], preferred_element_type=F32)


def _moe_combine_kernel(dest_ref, x_ref, meta_ref, gfin_ref, y_ref, o_ref, ybuf, sem, *, tm, M, final):
    m = pl.program_id(0)
    slot = m % 2

    def fetch(tile, s):
        base = tile * tm

        def body(r, c):
            for k in range(2):
                d = dest_ref[k * M + base + r]
                pltpu.make_async_copy(y_ref.at[pl.ds(d, 1)], ybuf.at[s, k, pl.ds(r, 1)], sem.at[s]).start(priority=k)
            return c

        lax.fori_loop(0, tm, body, 0, unroll=8)

    @pl.when(m == 0)
    def _():
        fetch(0, 0)

    @pl.when(m + 1 < pl.num_programs(0))
    def _():
        fetch(m + 1, 1 - slot)

    for k in range(2):
        pltpu.make_async_copy(y_ref.at[pl.ds(0, tm)], ybuf.at[slot, k], sem.at[slot]).wait()
    out = x_ref[...] + (meta_ref[:, 4:5] * ybuf[slot, 0] + meta_ref[:, 5:6] * ybuf[slot, 1])
    o_ref[...] = _rms(out, gfin_ref[...]) if final else out


def _moe_routed(x, g, rw, rb, w1, w3, w2, gfin=None, tm=512, tmd=256):
    M, D = x.shape
    TM = MOE_TM
    NT = 2 * M // TM + N_EXPERTS
    R = NT * TM
    xspec = pl.BlockSpec((tm, D), lambda m: (m, 0))
    meta, meta_t, cnt = pl.pallas_call(
        functools.partial(_moe_route_kernel, tm=tm), grid=(M // tm,),
        in_specs=[xspec] + _specs([g, rw, rb]),
        out_specs=[pl.BlockSpec((tm, LANES), lambda m: (m, 0)), pl.BlockSpec((8, tm), lambda m: (0, m)),
                   _full((1, LANES))],
        out_shape=[jax.ShapeDtypeStruct((M, LANES), F32), jax.ShapeDtypeStruct((8, M), F32),
                   jax.ShapeDtypeStruct((1, LANES), F32)],
        scratch_shapes=[pltpu.VMEM((1, LANES), F32)], compiler_params=_cparams("arbitrary"), name="moe_route",
    )(*_arrs([x, g, rw, rb]))
    cnt = cnt[0, :N_EXPERTS].astype(jnp.int32)
    ntile = (cnt + TM - 1) // TM
    tile_end = jnp.cumsum(ntile)
    tile_start = tile_end - ntile
    total = tile_end[-1]
    s = jnp.arange(NT, dtype=jnp.int32)
    tb = jnp.maximum(jnp.minimum(s, total - 1), 0)
    te = jnp.minimum(jnp.sum((tb[:, None] >= tile_end[None, :]).astype(jnp.int32), axis=1), N_EXPERTS - 1)
    nv = jnp.clip(cnt[te] - (tb - tile_start[te]) * TM, 0, TM).astype(jnp.int32)
    row_off = (tile_start * TM).astype(jnp.int32)
    sel = meta_t[0:4].astype(jnp.int32)
    dest = (row_off[sel[0:2]] + sel[2:4]).reshape(2 * M)
    xdspec = pl.BlockSpec((tmd, D), lambda m, *_: (m, 0))
    xg = pl.pallas_call(
        functools.partial(_moe_dispatch_kernel, tm=tmd, M=M),
        grid_spec=pltpu.PrefetchScalarGridSpec(
            num_scalar_prefetch=1, grid=(M // tmd,),
            in_specs=[xdspec, g.spec],
            out_specs=pl.BlockSpec(memory_space=pl.ANY),
            scratch_shapes=[pltpu.VMEM((2, tmd, D), F32), pltpu.SemaphoreType.DMA((2,))]),
        out_shape=jax.ShapeDtypeStruct((R, D), F32), compiler_params=_cparams("arbitrary"), name="moe_dispatch",
    )(dest, x, g.arr)
    pick = lambda s, te, tb, nv, tot: te[s]
    tspec = pl.BlockSpec((TM, D), lambda s, te, tb, nv, tot: (tb[s], 0))
    y = pl.pallas_call(
        _moe_group_kernel,
        grid_spec=pltpu.PrefetchScalarGridSpec(
            num_scalar_prefetch=4, grid=(NT,),
            in_specs=[tspec] + [_expert_spec(w, pick) for w in (w1, w3, w2)],
            out_specs=tspec),
        out_shape=jax.ShapeDtypeStruct((R, D), F32), compiler_params=_cparams("arbitrary"), name="moe_group",
    )(te, tb, nv, total.reshape(1).astype(jnp.int32), xg, w1.arr, w3.arr, w2.arr)
    final = gfin is not None
    gfin = g if gfin is None else gfin
    return pl.pallas_call(
        functools.partial(_moe_combine_kernel, tm=tmd, M=M, final=final),
        grid_spec=pltpu.PrefetchScalarGridSpec(
            num_scalar_prefetch=1, grid=(M // tmd,),
            in_specs=[xdspec, pl.BlockSpec((tmd, LANES), lambda m, *_: (m, 0)), gfin.spec,
                      pl.BlockSpec(memory_space=pl.ANY)],
            out_specs=xdspec,
            scratch_shapes=[pltpu.VMEM((2, 2, tmd, D), F32), pltpu.SemaphoreType.DMA((2,))]),
        out_shape=jax.ShapeDtypeStruct((M, D), F32), compiler_params=_cparams("arbitrary"), name="moe_combine",
    )(dest, x, meta, gfin.arr, y)


def _final_norm_kernel(x_ref, g_ref, o_ref):
    o_ref[...] = _rms(x_ref[...], g_ref[...])


def _final_norm(x, g, tm):
    M, D = x.shape
    spec = pl.BlockSpec((tm, D), lambda m: (m, 0))
    return pl.pallas_call(
        _final_norm_kernel, grid=(M // tm,), in_specs=[spec] + _specs([g]), out_specs=spec,
        out_shape=jax.ShapeDtypeStruct(x.shape, F32), compiler_params=_cparams("parallel"), name="final_norm",
    )(*_arrs([x, g]))


def kernel(x_prompt, x_sample, state_ssm, state_ssm_conv, state_ret, state_pool, state_conv, cache_mem_k, cache_mem_v, mem_prompt, norm_mix, norm_cross, norm_ffn, norm_mem, norm_final, w_in_even, ssd_conv_w, ssd_conv_b, ssd_dt_bias, ssd_a_log, ssd_d, ssd_norm, w_out_even, ffn_w1, ffn_w3, ffn_w2, w_in_odd, pool_w, pool_scale, conv_w, conv_b, conv_norm_g, conv_norm_b, w_out_odd, router_w, router_b, moe_w1, moe_w3, moe_w2, xattn_q, xattn_k, xattn_v, xattn_o):
    p = dict(norm_mix=norm_mix, norm_cross=norm_cross, norm_ffn=norm_ffn, norm_mem=norm_mem,
             w_in_even=w_in_even, ssd_conv_w=ssd_conv_w, ssd_conv_b=ssd_conv_b, ssd_dt_bias=ssd_dt_bias,
             ssd_a_log=ssd_a_log, ssd_d=ssd_d, ssd_norm=ssd_norm, w_out_even=w_out_even,
             ffn_w1=ffn_w1, ffn_w3=ffn_w3, ffn_w2=ffn_w2,
             w_in_odd=w_in_odd, pool_w=pool_w, pool_scale=pool_scale, conv_w=conv_w, conv_b=conv_b,
             conv_norm_g=conv_norm_g, conv_norm_b=conv_norm_b, w_out_odd=w_out_odd,
             router_w=router_w, router_b=router_b, moe_w1=moe_w1, moe_w3=moe_w3, moe_w2=moe_w2,
             xattn_q=xattn_q, xattn_k=xattn_k, xattn_v=xattn_v, xattn_o=xattn_o)
    B, L, D = x_prompt.shape
    S = x_sample.shape[0]
    rconsts = _ret_consts()
    cos_p, sin_p = _rope_tables(np.arange(L, dtype=np.float32))
    cos_s, sin_s = _rope_tables(np.float32(PAST_LEN) + np.arange(1, dtype=np.float32))
    gam = _pad_lanes(jnp.exp(rconsts[4]))

    mem_k, mem_v = _mem_kv(mem_prompt.reshape(B * MEM_LEN, D), norm_mem,
                           jnp.concatenate([xattn_k, xattn_v], axis=2).astype(BF16))
    mem_k = mem_k.reshape(DEPTH, B, MEM_LEN, D)
    mem_v = mem_v.reshape(DEPTH, B, MEM_LEN, D)
    cache_k = _cache_tiles(cache_mem_k)
    cache_v = _cache_tiles(cache_mem_v)
    sconv_rows = state_ssm_conv.transpose(0, 2, 1, 3)
    ssm_rows = state_ssm.swapaxes(3, 4)
    chain = None
    pool_rows = state_pool.transpose(0, 2, 1, 3)
    conv_rows = state_conv.transpose(0, 2, 1, 3)

    xp = x_prompt
    xs = x_sample.reshape(S, D)
    outs_p = dict(ssm=[], ssm_conv=[], ret=[], pool=[], conv=[])
    outs_s = dict(ssm=[], ssm_conv=[], ret=[], pool=[], conv=[])
    ep, op, mp = _even_params(p), _odd_params(p), _moe_params(p)
    fp = (ffn_w1.astype(BF16), ffn_w3.astype(BF16), ffn_w2.astype(BF16))
    wq_all, wo_all = xattn_q.astype(BF16), xattn_o.astype(BF16)
    wq_perm, wo_perm = _xattn_perm_weights(wq_all, wo_all)
    n_mix, n_cross, n_ffn = _rows(norm_mix), _rows(norm_cross), _rows(norm_ffn)
    g_fin = _Layer(_rows(norm_final[None]), 0)
    for l in range(DEPTH):
        i = l // 2
        g_mix = _Layer(n_mix, l)
        if l % 2 == 0:
            ew = _even_layer_params(ep, i)
            xp, c_st, s_st, r_st = _even_prompt(xp, g_mix, *ew, cos_p, sin_p, rconsts)
            outs_p['ssm_conv'].append(c_st); outs_p['ssm'].append(s_st); outs_p['ret'].append(r_st)
            xs, c_st, *chain = _even_sample(xs, g_mix, *ew, cos_s, sin_s, gam,
                                            sconv_rows, ssm_rows, state_ret, i, chain)
            outs_s['ssm_conv'].append(c_st)
        else:
            ow = _odd_layer_params(op, i)
            xp, p_st, c_st = _odd_prompt(xp, g_mix, *ow)
            outs_p['pool'].append(p_st); outs_p['conv'].append(c_st)
            xs, p_st, c_st = _odd_sample(xs, g_mix, *ow, pool_rows, conv_rows, i)
            outs_s['pool'].append(p_st); outs_s['conv'].append(c_st)
        g_x = _Layer(n_cross, l)
        xp = _xattn_prompt(xp, g_x, _Layer(wq_all, l), mem_k, mem_v, _Layer(wo_all, l), l)
        xs = _xattn_sample(xs, g_x, _Layer(wq_perm, l), cache_k, cache_v, _Layer(wo_perm, l), l)
        g_f = _Layer(n_ffn, l)
        if l % 2 == 0:
            fw = tuple(_Layer(a, i) for a in fp)
            xp = _ffn(xp.reshape(B * L, D), g_f, *fw, tm=512).reshape(B, L, D)
            xs = _ffn(xs, g_f, *fw, tm=S)
        else:
            mw = _moe_layer_params(mp, i)
            gfin = g_fin if l == DEPTH - 1 else None
            xp = _moe_routed(xp.reshape(B * L, D), g_f, *mw, gfin=gfin).reshape(B, L, D)
            xs = _moe(xs, g_f, *mw, tm=S)
    y_prompt = xp
    y_sample = _final_norm(xs, g_fin, tm=S).reshape(S, 1, D)
    st = lambda d, k: jnp.stack(d[k])
    st_rows = lambda d, k: jnp.stack(d[k]).transpose(0, 2, 1, 3)
    return (y_prompt, y_sample,
            mem_k.reshape(DEPTH, B, MEM_LEN, X_HEADS, X_HEAD_DIM), mem_v.reshape(DEPTH, B, MEM_LEN, X_HEADS, X_HEAD_DIM),
            st(outs_p, 'ssm'), st(outs_p, 'ssm_conv'), st(outs_p, 'ret'), st(outs_p, 'pool'), st(outs_p, 'conv'),
            chain[0].swapaxes(3, 4), st_rows(outs_s, 'ssm_conv'), chain[1], st_rows(outs_s, 'pool'),
            st_rows(outs_s, 'conv'))
```
